```python
import math
import jax, jax.numpy as jnp
from jax import lax
import numpy as np

D_MODEL = 1024
BATCH = 32
SEQ = 2048
DEPTH = 4

GRID_W = 64
CTX_LEN = 256
N_EVEN = (DEPTH + 1) // 2
N_ODD = DEPTH // 2
D_FF = 4 * D_MODEL
Q_BLOCK = 128
ROPE_THETA = 10000.0
RMS_EPS = 1e-6
N_MOD = 6

HEAD_DIM = 64
DIFF_QK_DIM = HEAD_DIM
DIFF_V_DIM = 2 * HEAD_DIM
DIFF_HEADS = (D_MODEL // 2) // DIFF_V_DIM
DIFF_SCALE = DIFF_QK_DIM ** -0.5
GQA_DIM = HEAD_DIM
GQA_Q_HEADS = (D_MODEL // 2) // GQA_DIM
GQA_KV_HEADS = 2
GQA_GROUP = GQA_Q_HEADS // GQA_KV_HEADS
GQA_SCALE = GQA_DIM ** -0.5
AB_SPLITS = (DIFF_HEADS * 2 * DIFF_QK_DIM, DIFF_HEADS * 2 * DIFF_QK_DIM, DIFF_HEADS * DIFF_V_DIM,
             GQA_Q_HEADS * GQA_DIM, GQA_KV_HEADS * GQA_DIM, GQA_KV_HEADS * GQA_DIM)
AB_IN_W = sum(AB_SPLITS)
AB_OUT_W = DIFF_HEADS * DIFF_V_DIM + GQA_Q_HEADS * GQA_DIM
MLA_NOPE = 64
MLA_ROPE = 32
MLA_QK_DIM = MLA_NOPE + MLA_ROPE
MLA_V = 64
MLA_HEADS = D_MODEL // MLA_V
MLA_Q_RANK = D_MODEL // 2
MLA_KV_RANK = D_MODEL // 4
MLA_SPLITS = (MLA_Q_RANK, MLA_KV_RANK, MLA_ROPE)
MLA_IN_W = sum(MLA_SPLITS)
MLA_OUT_W = MLA_HEADS * MLA_V
MLA_SCALE = MLA_QK_DIM ** -0.5

kernel_name = 'hybrid_diffattn_gqa_mla_prefix_dit'


def rmsnorm(x, g):
    xf = x.astype(jnp.float32)
    y = xf * lax.rsqrt(jnp.mean(xf * xf, axis=-1, keepdims=True) + RMS_EPS)
    return (y * g.astype(jnp.float32)).astype(x.dtype)


def modulate(h, shift, scale):
    return h * (1 + scale) + shift


def lambda_init(layer):
    return 0.8 - 0.6 * math.exp(-0.3 * layer)


def split_cols(t, widths):
    idx = np.cumsum(widths)[:-1].tolist()
    return jnp.split(t, idx, axis=-1)


def axial_rope(n_tokens, rot_dim):
    rows = n_tokens // GRID_W
    n_freq = rot_dim // 4
    inv_freq = ROPE_THETA ** (-jnp.arange(n_freq, dtype=jnp.float32) / n_freq)
    row = jnp.repeat(jnp.arange(rows, dtype=jnp.float32), GRID_W)
    col = jnp.tile(jnp.arange(GRID_W, dtype=jnp.float32), rows)
    ang = jnp.concatenate([row[:, None] * inv_freq, col[:, None] * inv_freq], axis=-1)
    return jnp.cos(ang), jnp.sin(ang)


def apply_rope(x, cos, sin):
    half = x.shape[-1] // 2
    x1, x2 = x[..., :half], x[..., half:]
    cos, sin = cos.astype(x.dtype), sin.astype(x.dtype)
    return jnp.concatenate([x1 * cos - x2 * sin, x2 * cos + x1 * sin], axis=-1)


def blockwise(fn, q):
    *lead, n_q, d = q.shape
    nb = n_q // Q_BLOCK
    qb = jnp.moveaxis(q.reshape(*lead, nb, Q_BLOCK, d), -3, 0)
    ob = jnp.moveaxis(lax.map(fn, qb), 0, -3)
    return ob.reshape(*ob.shape[:-3], n_q, ob.shape[-1])


def gqa_attend(q, k, v, scale):
    s = jnp.einsum('bhgqd,bhkd->bhgqk', q, k).astype(jnp.float32) * scale
    p = jax.nn.softmax(s, axis=-1).astype(v.dtype)
    return jnp.einsum('bhgqk,bhkd->bhgqd', p, v)


def diff_attend(q, k, v, lam):
    s = jnp.einsum('bhmqd,bhmkd->bhmqk', q, k).astype(jnp.float32) * DIFF_SCALE
    p = jax.nn.softmax(s, axis=-1)
    a = (p[:, :, 0] - lam * p[:, :, 1]).astype(v.dtype)
    return jnp.einsum('bhqk,bhkd->bhqd', a, v)


def prefix_attention(attend, q_lat, k_lat, v_lat, q_ctx, k_ctx, v_ctx, compute_ctx):
    k_all = jnp.concatenate([k_ctx, k_lat], axis=-2)
    v_all = jnp.concatenate([v_ctx, v_lat], axis=-2)
    out_lat = blockwise(lambda qb: attend(qb, k_all, v_all), q_lat)
    out_ctx = attend(q_ctx, k_ctx, v_ctx) if compute_ctx else None
    return out_lat, out_ctx


def diff_gqa_mixer(h_lat, h_ctx, rope, w_in, w_out, diff_qk_norm, diff_lambda, diff_subln,
                   gqa_qk_norm, lam_init, compute_ctx):
    def project(h, rope_cs):
        b, n, _ = h.shape
        aq, ak, av, bq, bk, bv = split_cols(h @ w_in, AB_SPLITS)
        aq = rmsnorm(aq.reshape(b, n, DIFF_HEADS, 2, DIFF_QK_DIM).transpose(0, 2, 3, 1, 4), diff_qk_norm[0])
        ak = rmsnorm(ak.reshape(b, n, DIFF_HEADS, 2, DIFF_QK_DIM).transpose(0, 2, 3, 1, 4), diff_qk_norm[1])
        av = av.reshape(b, n, DIFF_HEADS, DIFF_V_DIM).transpose(0, 2, 1, 3)
        bq = rmsnorm(bq.reshape(b, n, GQA_KV_HEADS, GQA_GROUP, GQA_DIM).transpose(0, 2, 3, 1, 4), gqa_qk_norm[0])
        bk = rmsnorm(bk.reshape(b, n, GQA_KV_HEADS, GQA_DIM).transpose(0, 2, 1, 3), gqa_qk_norm[1])
        bv = bv.reshape(b, n, GQA_KV_HEADS, GQA_DIM).transpose(0, 2, 1, 3)
        if rope_cs is not None:
            cos, sin = rope_cs
            aq, ak, bq, bk = (apply_rope(t, cos, sin) for t in (aq, ak, bq, bk))
        return aq, ak, av, bq, bk, bv

    aq, ak, av, bq, bk, bv = project(h_lat, rope)
    caq, cak, cav, cbq, cbk, cbv = project(h_ctx, None)
    lq1, lk1, lq2, lk2 = diff_lambda.astype(jnp.float32)
    lam = jnp.exp(jnp.sum(lq1 * lk1)) - jnp.exp(jnp.sum(lq2 * lk2)) + lam_init
    a_lat, a_ctx = prefix_attention(lambda q, k, v: diff_attend(q, k, v, lam),
                                    aq, ak, av, caq, cak, cav, compute_ctx)
    b_lat, b_ctx = prefix_attention(lambda q, k, v: gqa_attend(q, k, v, GQA_SCALE),
                                    bq, bk, bv, cbq, cbk, cbv, compute_ctx)

    def merge(a, bo):
        bsz, _, n, _ = a.shape
        a = (rmsnorm(a, diff_subln) * (1.0 - lam_init)).transpose(0, 2, 1, 3).reshape(bsz, n, DIFF_HEADS * DIFF_V_DIM)
        bo = bo.transpose(0, 3, 1, 2, 4).reshape(bsz, n, GQA_Q_HEADS * GQA_DIM)
        return jnp.concatenate([a, bo], axis=-1) @ w_out

    y_ctx = merge(a_ctx, b_ctx) if compute_ctx else None
    return merge(a_lat, b_lat), y_ctx


def mla_mixer(h_lat, h_ctx, rope, w_in, q_norm, w_q_up, kv_norm, w_kv_up, qk_norm, w_out, compute_ctx):
    def project(h, rope_cs):
        b, n, _ = h.shape
        q_c, kv_c, k_r = split_cols(h @ w_in, MLA_SPLITS)
        q = (rmsnorm(q_c, q_norm) @ w_q_up).reshape(b, n, MLA_HEADS, MLA_QK_DIM).transpose(0, 2, 1, 3)
        kv = (rmsnorm(kv_c, kv_norm) @ w_kv_up).reshape(b, n, MLA_HEADS, MLA_NOPE + MLA_V).transpose(0, 2, 1, 3)
        k_nope, v = kv[..., :MLA_NOPE], kv[..., MLA_NOPE:]
        k_rope = jnp.broadcast_to(k_r[:, None], (b, MLA_HEADS, n, MLA_ROPE))
        k = jnp.concatenate([k_nope, k_rope], axis=-1)
        q = rmsnorm(q, qk_norm[0])
        k = rmsnorm(k, qk_norm[1])
        if rope_cs is not None:
            cos, sin = rope_cs
            q = jnp.concatenate([q[..., :MLA_NOPE], apply_rope(q[..., MLA_NOPE:], cos, sin)], axis=-1)
            k = jnp.concatenate([k[..., :MLA_NOPE], apply_rope(k[..., MLA_NOPE:], cos, sin)], axis=-1)
        return q[:, :, None], k, v

    q, k, v = project(h_lat, rope)
    cq, ck, cv = project(h_ctx, None)
    o_lat, o_ctx = prefix_attention(lambda qq, kk, vv: gqa_attend(qq, kk, vv, MLA_SCALE),
                                    q, k, v, cq, ck, cv, compute_ctx)

    def out(o):
        bsz, _, _, n, _ = o.shape
        return o[:, :, 0].transpose(0, 2, 1, 3).reshape(bsz, n, MLA_OUT_W) @ w_out

    y_ctx = out(o_ctx) if compute_ctx else None
    return out(o_lat), y_ctx


def sqrelu_mlp(h, w1, w2):
    return jnp.square(jax.nn.relu(h @ w1)) @ w2


def setup_inputs(seed: int = 0) -> dict:
    key = jax.random.key(seed)
    ks = jax.random.split(key, 23)
    f32 = jnp.float32

    def w(k, shape, fan_in, gain=1.0):
        return jax.random.normal(k, shape, f32) * (gain * fan_in ** -0.5)

    def g(k, shape):
        return 1.0 + 0.02 * jax.random.normal(k, shape, f32)

    return {
        'x': jax.random.normal(ks[0], (BATCH, SEQ, D_MODEL), f32),
        'c': jax.random.normal(ks[1], (BATCH, D_MODEL), f32),
        'ctx': jax.random.normal(ks[2], (BATCH, CTX_LEN, D_MODEL), f32),
        'c_ctx': jax.random.normal(ks[3], (D_MODEL,), f32),
        'ada_w': w(ks[4], (DEPTH, D_MODEL, N_MOD * D_MODEL), D_MODEL, 0.5),
        'ada_b': 0.02 * jax.random.normal(ks[5], (DEPTH, N_MOD * D_MODEL), f32),
        'norm_mix': g(ks[6], (DEPTH, D_MODEL)),
        'norm_mlp': g(ks[7], (DEPTH, D_MODEL)),
        'mlp_w1': w(ks[8], (DEPTH, D_MODEL, D_FF), D_MODEL),
        'mlp_w2': w(ks[9], (DEPTH, D_FF, D_MODEL), D_FF),
        'ab_w_in': w(ks[10], (N_EVEN, D_MODEL, AB_IN_W), D_MODEL),
        'ab_w_out': w(ks[11], (N_EVEN, AB_OUT_W, D_MODEL), AB_OUT_W),
        'diff_qk_norm': g(ks[12], (N_EVEN, 2, DIFF_QK_DIM)),
        'diff_lambda': 0.1 * jax.random.normal(ks[13], (N_EVEN, 4, DIFF_QK_DIM), f32),
        'diff_subln': g(ks[14], (N_EVEN, DIFF_V_DIM)),
        'gqa_qk_norm': g(ks[15], (N_EVEN, 2, GQA_DIM)),
        'mla_w_in': w(ks[16], (N_ODD, D_MODEL, MLA_IN_W), D_MODEL),
        'mla_q_norm': g(ks[17], (N_ODD, MLA_Q_RANK)),
        'mla_w_q_up': w(ks[18], (N_ODD, MLA_Q_RANK, MLA_HEADS * MLA_QK_DIM), MLA_Q_RANK),
        'mla_kv_norm': g(ks[19], (N_ODD, MLA_KV_RANK)),
        'mla_w_kv_up': w(ks[20], (N_ODD, MLA_KV_RANK, MLA_HEADS * (MLA_NOPE + MLA_V)), MLA_KV_RANK),
        'mla_qk_norm': g(ks[21], (N_ODD, 2, MLA_QK_DIM)),
        'mla_w_out': w(ks[22], (N_ODD, MLA_OUT_W, D_MODEL), MLA_OUT_W),
    }


def reference(x, c, ctx, c_ctx, ada_w, ada_b, norm_mix, norm_mlp, mlp_w1, mlp_w2,
              ab_w_in, ab_w_out, diff_qk_norm, diff_lambda, diff_subln, gqa_qk_norm,
              mla_w_in, mla_q_norm, mla_w_q_up, mla_kv_norm, mla_w_kv_up, mla_qk_norm, mla_w_out):
    n_lat = x.shape[1]
    rope_hd = axial_rope(n_lat, HEAD_DIM)
    rope_mla = axial_rope(n_lat, MLA_ROPE)
    silu_c = jax.nn.silu(c)
    silu_cc = jax.nn.silu(c_ctx)
    for layer in range(DEPTH):
        compute_ctx = layer < DEPTH - 1
        i = layer // 2
        mod_lat = (silu_c @ ada_w[layer] + ada_b[layer])[:, None, :]
        mod_ctx = silu_cc @ ada_w[layer] + ada_b[layer]
        sh1, sc1, g1, sh2, sc2, g2 = jnp.split(mod_lat, N_MOD, axis=-1)
        csh1, csc1, cg1, csh2, csc2, cg2 = jnp.split(mod_ctx, N_MOD, axis=-1)
        h_lat = modulate(rmsnorm(x, norm_mix[layer]), sh1, sc1)
        h_ctx = modulate(rmsnorm(ctx, norm_mix[layer]), csh1, csc1)
        if layer % 2 == 0:
            y_lat, y_ctx = diff_gqa_mixer(h_lat, h_ctx, rope_hd, ab_w_in[i], ab_w_out[i], diff_qk_norm[i],
                                          diff_lambda[i], diff_subln[i], gqa_qk_norm[i],
                                          lambda_init(layer), compute_ctx)
        else:
            y_lat, y_ctx = mla_mixer(h_lat, h_ctx, rope_mla, mla_w_in[i], mla_q_norm[i], mla_w_q_up[i],
                                     mla_kv_norm[i], mla_w_kv_up[i], mla_qk_norm[i], mla_w_out[i],
                                     compute_ctx)
        x = x + g1 * y_lat
        x = x + g2 * sqrelu_mlp(modulate(rmsnorm(x, norm_mlp[layer]), sh2, sc2), mlp_w1[layer], mlp_w2[layer])
        if compute_ctx:
            ctx = ctx + cg1 * y_ctx
            ctx = ctx + cg2 * sqrelu_mlp(modulate(rmsnorm(ctx, norm_mlp[layer]), csh2, csc2),
                                         mlp_w1[layer], mlp_w2[layer])
    return x
```

```python
import functools
import math

import jax
import jax.numpy as jnp
from jax import lax
from jax.experimental import pallas as pl
from jax.experimental.pallas import tpu as pltpu

F32 = jnp.float32
BF16 = jnp.bfloat16

D_MODEL = 1024
DEPTH = 4
GRID_W = 64
D_FF = 4 * D_MODEL
N_MOD = 6
ROPE_THETA = 10000.0
RMS_EPS = 1e-6

HEAD_DIM = 64
DIFF_HEADS = 4
DIFF_V_DIM = 128
GQA_Q_HEADS = 8
GQA_KV_HEADS = 2
GQA_GROUP = GQA_Q_HEADS // GQA_KV_HEADS
AB_IN_W = 2304
QK_SCALE_64 = HEAD_DIM ** -0.5

MLA_NOPE = 64
MLA_ROPE = 32
MLA_QK_DIM = MLA_NOPE + MLA_ROPE
MLA_V = 64
MLA_HEADS = 16
MLA_Q_RANK = 512
MLA_KV_RANK = 256
MLA_IN_W = MLA_Q_RANK + MLA_KV_RANK + MLA_ROPE
MLA_SCALE = MLA_QK_DIM ** -0.5

TOKEN_TILE = 256
MOD_ROWS = 40
FF_CHUNK = 1024
VMEM_LIMIT_BYTES = 56 * 1024 * 1024

_NT = (((1,), (1,)), ((), ()))
_TN = (((0,), (0,)), ((), ()))


def _lambda_init(layer):
    return 0.8 - 0.6 * math.exp(-0.3 * layer)


def _params(n_axes):
    return pltpu.CompilerParams(
        dimension_semantics=("arbitrary",) * n_axes, vmem_limit_bytes=VMEM_LIMIT_BYTES)


def _const_spec(shape):
    return pl.BlockSpec(shape, lambda *_: (0,) * len(shape))


def _rms_modulate(x, gain, shift, scale):
    ms = jnp.mean(x * x, axis=-1, keepdims=True)
    return (x * lax.rsqrt(ms + RMS_EPS) * gain) * (1.0 + scale) + shift


def _rms_rows(y, n):
    return lax.rsqrt(jnp.sum(y * y, axis=0, keepdims=True) * (1.0 / n) + RMS_EPS)


def _rope_rows(x1, x2, cos, sin):
    return x1 * cos - x2 * sin, x2 * cos + x1 * sin


def _mod_kernel(a_ref, w_ref, b_ref, o_ref):
    a = a_ref[...]
    a = a / (1.0 + jnp.exp(-a))
    o_ref[0] = jnp.dot(a.astype(BF16), w_ref[0].astype(BF16), preferred_element_type=F32) + b_ref[0]


def _modulation(c, c_ctx, ada_w, ada_b):
    b = c.shape[0]
    rows = jnp.concatenate([c, c_ctx[None, :], jnp.zeros((MOD_ROWS - b - 1, D_MODEL), F32)], axis=0)
    tn = 1024
    out = pl.pallas_call(
        _mod_kernel,
        grid=(DEPTH, N_MOD * D_MODEL // tn),
        in_specs=[
            pl.BlockSpec((MOD_ROWS, D_MODEL), lambda l, j: (0, 0)),
            pl.BlockSpec((1, D_MODEL, tn), lambda l, j: (l, 0, j)),
            pl.BlockSpec((1, 1, tn), lambda l, j: (l, 0, j)),
        ],
        out_specs=pl.BlockSpec((1, MOD_ROWS, tn), lambda l, j: (l, 0, j)),
        out_shape=jax.ShapeDtypeStruct((DEPTH, MOD_ROWS, N_MOD * D_MODEL), F32),
        compiler_params=_params(2),
        name="adaln_modulation",
    )(rows, ada_w, ada_b[:, None, :])
    return out.reshape(DEPTH, MOD_ROWS, N_MOD, D_MODEL)


def _mod_spec(layer, batch, tile_off):
    def index(b, t):
        return (layer, jnp.where(t + tile_off == 0, batch, b), 0, 0)
    return pl.BlockSpec((None, None, N_MOD, D_MODEL), index)


def _inproj_even_kernel(x_ref, mod_ref, g_ref, w_ref, gain_ref, cos_ref, sin_ref, q_ref, k_ref, v_ref):
    h = _rms_modulate(x_ref[0], g_ref[...], mod_ref[0:1, :], mod_ref[1:2, :]).astype(BF16)
    cos = cos_ref[...]
    sin = sin_ref[...]

    def proj(r0, r1):
        return lax.dot_general(w_ref[r0:r1, :], h, _NT, preferred_element_type=F32)

    def qk_heads(r0, n_heads, dst_ref, d0):
        y = proj(r0, r0 + n_heads * HEAD_DIM)
        for i in range(n_heads):
            yh = y[i * HEAD_DIM:(i + 1) * HEAD_DIM]
            gcol = gain_ref[r0 + i * HEAD_DIM:r0 + (i + 1) * HEAD_DIM, :]
            yn = yh * _rms_rows(yh, HEAD_DIM) * gcol
            o1, o2 = _rope_rows(yn[0:32], yn[32:64], cos, sin)
            base = d0 + i * HEAD_DIM
            dst_ref[0, base:base + 32, :] = o1.astype(BF16)
            dst_ref[0, base + 32:base + 64, :] = o2.astype(BF16)

    qk_heads(0, 8, q_ref, 0)
    qk_heads(512, 8, k_ref, 0)
    v_ref[0, 0:512, :] = proj(1024, 1536).astype(BF16)
    qk_heads(1536, 8, q_ref, 512)
    qk_heads(2048, 2, k_ref, 512)
    v_ref[0, 512:640, :] = proj(2176, 2304).astype(BF16)


def _inproj_even(xall, mod, layer, g, w_t, gains, cos_t, sin_t):
    b, lt, _ = xall.shape
    nt = lt // TOKEN_TILE
    tok = lambda rows: pl.BlockSpec((1, rows, TOKEN_TILE), lambda i, t: (i, 0, t))
    return pl.pallas_call(
        _inproj_even_kernel,
        grid=(b, nt),
        in_specs=[
            pl.BlockSpec((1, TOKEN_TILE, D_MODEL), lambda i, t: (i, t, 0)),
            _mod_spec(layer, b, 0),
            _const_spec((1, D_MODEL)),
            _const_spec((AB_IN_W, D_MODEL)),
            _const_spec((AB_IN_W, 1)),
            pl.BlockSpec((32, TOKEN_TILE), lambda i, t: (0, t)),
            pl.BlockSpec((32, TOKEN_TILE), lambda i, t: (0, t)),
        ],
        out_specs=[tok(1024), tok(640), tok(640)],
        out_shape=[jax.ShapeDtypeStruct((b, 1024, lt), BF16),
                   jax.ShapeDtypeStruct((b, 640, lt), BF16),
                   jax.ShapeDtypeStruct((b, 640, lt), BF16)],
        compiler_params=_params(2),
        name="inproj_even",
    )(xall, mod, g, w_t, gains, cos_t, sin_t)


def _exp_scores_t(k_t, q_t):
    s = lax.dot_general(k_t, q_t, _TN, preferred_element_type=F32)
    e = jnp.exp(s - jnp.max(s, axis=0, keepdims=True))
    return e, jnp.sum(e, axis=0, keepdims=True)


def _attn_even_body(nk, q_ref, k_ref, v_ref, dl_ref, sub_ref, o_ref, *, lam_init):
    dl = dl_ref[...]
    lam = (jnp.exp(jnp.sum(dl[0:1] * dl[1:2], axis=1, keepdims=True))
           - jnp.exp(jnp.sum(dl[2:3] * dl[3:4], axis=1, keepdims=True)) + lam_init)
    for h in range(DIFF_HEADS):
        r = 2 * h * HEAD_DIM
        e1, l1 = _exp_scores_t(k_ref[0, r:r + 64, 0:nk], q_ref[0, r:r + 64, :])
        e2, l2 = _exp_scores_t(k_ref[0, r + 64:r + 128, 0:nk], q_ref[0, r + 64:r + 128, :])
        a = e1 * (1.0 / l1) - e2 * (lam / l2)
        v0 = h * DIFF_V_DIM
        o = jnp.dot(v_ref[0, v0:v0 + DIFF_V_DIM, 0:nk], a.astype(BF16), preferred_element_type=F32)
        o = o * _rms_rows(o, DIFF_V_DIM) * sub_ref[...]
        o_ref[0, v0:v0 + DIFF_V_DIM, :] = o.astype(BF16)
    for j in range(GQA_KV_HEADS):
        k_t = k_ref[0, 512 + j * 64:512 + (j + 1) * 64, 0:nk]
        v_t = v_ref[0, 512 + j * 64:512 + (j + 1) * 64, 0:nk]
        for g in range(GQA_GROUP):
            r = 512 + (j * GQA_GROUP + g) * HEAD_DIM
            e, l = _exp_scores_t(k_t, q_ref[0, r:r + 64, :])
            o = jnp.dot(v_t, e.astype(BF16), preferred_element_type=F32) * (1.0 / l)
            o_ref[0, r:r + 64, :] = o.astype(BF16)


def _attn_kernel(body, n_ctx, n_all, tile_off, *refs):
    if tile_off == 0:
        t = pl.program_id(1)

        @pl.when(t == 0)
        def _():
            body(n_ctx, *refs)

        @pl.when(t > 0)
        def _():
            body(n_all, *refs)
    else:
        body(n_all, *refs)


def _attention(body, q_t, k_t, v_t, extras, extra_specs, out_rows, tile_off, name):
    b, _, lt = q_t.shape
    nt = lt // TOKEN_TILE - tile_off
    tile = lambda rows: pl.BlockSpec((1, rows, TOKEN_TILE), lambda i, t: (i, 0, t + tile_off))
    full = lambda rows: pl.BlockSpec((1, rows, lt), lambda i, t: (i, 0, 0))
    return pl.pallas_call(
        functools.partial(_attn_kernel, body, TOKEN_TILE, lt, tile_off),
        grid=(b, nt),
        in_specs=[tile(q_t.shape[1]), full(k_t.shape[1]), full(v_t.shape[1])] + extra_specs,
        out_specs=tile(out_rows),
        out_shape=jax.ShapeDtypeStruct((b, out_rows, lt), BF16),
        compiler_params=_params(2),
        name=name,
    )(q_t, k_t, v_t, *extras)


def _inproj_odd_kernel(x_ref, mod_ref, g_ref, w_ref, qn_ref, wq_ref, kvn_ref, wkv_ref, qg_ref, kg_ref,
                       cos_ref, sin_ref, q_ref, k_ref, v_ref):
    h = _rms_modulate(x_ref[0], g_ref[...], mod_ref[0:1, :], mod_ref[1:2, :]).astype(BF16)
    cos = cos_ref[...]
    sin = sin_ref[...]
    y = lax.dot_general(w_ref[...], h, _NT, preferred_element_type=F32)
    qc = y[0:MLA_Q_RANK]
    kvc = y[MLA_Q_RANK:MLA_Q_RANK + MLA_KV_RANK]
    kr = y[MLA_Q_RANK + MLA_KV_RANK:MLA_IN_W]
    qcn = (qc * _rms_rows(qc, MLA_Q_RANK) * qn_ref[...]).astype(BF16)
    kvcn = (kvc * _rms_rows(kvc, MLA_KV_RANK) * kvn_ref[...]).astype(BF16)
    q = jnp.dot(wq_ref[...], qcn, preferred_element_type=F32)
    kv = jnp.dot(wkv_ref[...], kvcn, preferred_element_type=F32)
    kr_ss = jnp.sum(kr * kr, axis=0, keepdims=True)
    qg = qg_ref[...]
    kg = kg_ref[...]
    for i in range(MLA_HEADS):
        b0 = i * MLA_QK_DIM
        qh = q[b0:b0 + MLA_QK_DIM]
        qn = qh * _rms_rows(qh, MLA_QK_DIM) * qg
        o1, o2 = _rope_rows(qn[64:80], qn[80:96], cos, sin)
        q_ref[0, b0:b0 + 64, :] = qn[0:64].astype(BF16)
        q_ref[0, b0 + 64:b0 + 80, :] = o1.astype(BF16)
        q_ref[0, b0 + 80:b0 + 96, :] = o2.astype(BF16)
        kn = kv[i * 128:i * 128 + MLA_NOPE]
        r = lax.rsqrt((jnp.sum(kn * kn, axis=0, keepdims=True) + kr_ss) * (1.0 / MLA_QK_DIM) + RMS_EPS)
        krn = kr * r * kg[64:96]
        o1, o2 = _rope_rows(krn[0:16], krn[16:32], cos, sin)
        k_ref[0, b0:b0 + 64, :] = (kn * r * kg[0:64]).astype(BF16)
        k_ref[0, b0 + 64:b0 + 80, :] = o1.astype(BF16)
        k_ref[0, b0 + 80:b0 + 96, :] = o2.astype(BF16)
        v_ref[0, i * MLA_V:(i + 1) * MLA_V, :] = kv[i * 128 + MLA_NOPE:(i + 1) * 128].astype(BF16)


def _inproj_odd(xall, mod, layer, g, w_t, qn, wq_t, kvn, wkv_t, qg, kg, cos_t, sin_t):
    b, lt, _ = xall.shape
    nt = lt // TOKEN_TILE
    tok = lambda rows: pl.BlockSpec((1, rows, TOKEN_TILE), lambda i, t: (i, 0, t))
    nq = MLA_HEADS * MLA_QK_DIM
    nv = MLA_HEADS * MLA_V
    return pl.pallas_call(
        _inproj_odd_kernel,
        grid=(b, nt),
        in_specs=[
            pl.BlockSpec((1, TOKEN_TILE, D_MODEL), lambda i, t: (i, t, 0)),
            _mod_spec(layer, b, 0),
            _const_spec((1, D_MODEL)),
            _const_spec((MLA_IN_W, D_MODEL)),
            _const_spec((MLA_Q_RANK, 1)),
            _const_spec((nq, MLA_Q_RANK)),
            _const_spec((MLA_KV_RANK, 1)),
            _const_spec((MLA_HEADS * (MLA_NOPE + MLA_V), MLA_KV_RANK)),
            _const_spec((MLA_QK_DIM, 1)),
            _const_spec((MLA_QK_DIM, 1)),
            pl.BlockSpec((16, TOKEN_TILE), lambda i, t: (0, t)),
            pl.BlockSpec((16, TOKEN_TILE), lambda i, t: (0, t)),
        ],
        out_specs=[tok(nq), tok(nq), tok(nv)],
        out_shape=[jax.ShapeDtypeStruct((b, nq, lt), BF16),
                   jax.ShapeDtypeStruct((b, nq, lt), BF16),
                   jax.ShapeDtypeStruct((b, nv, lt), BF16)],
        compiler_params=_params(2),
        name="inproj_odd",
    )(xall, mod, g, w_t, qn, wq_t, kvn, wkv_t, qg, kg, cos_t, sin_t)


def _attn_odd_body(nk, q_ref, k_ref, v_ref, o_ref):
    for i in range(MLA_HEADS):
        b0 = i * MLA_QK_DIM
        e, l = _exp_scores_t(k_ref[0, b0:b0 + MLA_QK_DIM, 0:nk], q_ref[0, b0:b0 + MLA_QK_DIM, :])
        o = jnp.dot(v_ref[0, i * MLA_V:(i + 1) * MLA_V, 0:nk], e.astype(BF16),
                    preferred_element_type=F32) * (1.0 / l)
        o_ref[0, i * MLA_V:(i + 1) * MLA_V, :] = o.astype(BF16)


def _out_mlp_kernel(x_ref, o_ref, mod_ref, g_ref, wo_ref, w1_ref, w2_ref, out_ref):
    y = lax.dot_general(o_ref[0], wo_ref[...], _TN, preferred_element_type=F32)
    x1 = x_ref[0] + mod_ref[2:3, :] * y
    h = _rms_modulate(x1, g_ref[...], mod_ref[3:4, :], mod_ref[4:5, :]).astype(BF16)
    acc = jnp.zeros(x1.shape, F32)
    for c in range(D_FF // FF_CHUNK):
        u = jnp.dot(h, w1_ref[:, c * FF_CHUNK:(c + 1) * FF_CHUNK], preferred_element_type=F32)
        a = jnp.square(jnp.maximum(u, 0.0)).astype(BF16)
        acc = acc + jnp.dot(a, w2_ref[c * FF_CHUNK:(c + 1) * FF_CHUNK, :], preferred_element_type=F32)
    out_ref[0] = x1 + mod_ref[5:6, :] * acc


def _out_mlp(xall, o_t, mod, layer, g, wo, w1, w2, tile_off):
    b, lt, _ = xall.shape
    nt = lt // TOKEN_TILE - tile_off
    return pl.pallas_call(
        _out_mlp_kernel,
        grid=(b, nt),
        in_specs=[
            pl.BlockSpec((1, TOKEN_TILE, D_MODEL), lambda i, t: (i, t + tile_off, 0)),
            pl.BlockSpec((1, D_MODEL, TOKEN_TILE), lambda i, t: (i, 0, t + tile_off)),
            _mod_spec(layer, b, tile_off),
            _const_spec((1, D_MODEL)),
            _const_spec((D_MODEL, D_MODEL)),
            _const_spec((D_MODEL, D_FF)),
            _const_spec((D_FF, D_MODEL)),
        ],
        out_specs=pl.BlockSpec((1, TOKEN_TILE, D_MODEL), lambda i, t: (i, t, 0)),
        out_shape=jax.ShapeDtypeStruct((b, nt * TOKEN_TILE, D_MODEL), F32),
        compiler_params=_params(2),
        name="outproj_mlp",
    )(xall, o_t, mod, g, wo, w1, w2)


def _rope_tables(n_lat, n_ctx, rot_dim):
    rows = n_lat // GRID_W
    n_freq = rot_dim // 4
    inv_freq = ROPE_THETA ** (-jnp.arange(n_freq, dtype=F32) / n_freq)
    row = jnp.repeat(jnp.arange(rows, dtype=F32), GRID_W)
    col = jnp.tile(jnp.arange(GRID_W, dtype=F32), rows)
    ang = jnp.concatenate([row[:, None] * inv_freq, col[:, None] * inv_freq], axis=-1)
    pad = ((0, 0), (n_ctx, 0))
    return (jnp.pad(jnp.cos(ang).T, pad, constant_values=1.0),
            jnp.pad(jnp.sin(ang).T, pad, constant_values=0.0))


def _col(v):
    return v.astype(F32)[:, None]


def kernel(x, c, ctx, c_ctx, ada_w, ada_b, norm_mix, norm_mlp, mlp_w1, mlp_w2, ab_w_in, ab_w_out,
           diff_qk_norm, diff_lambda, diff_subln, gqa_qk_norm, mla_w_in, mla_q_norm, mla_w_q_up,
           mla_kv_norm, mla_w_kv_up, mla_qk_norm, mla_w_out):
    b, n_lat, d = x.shape
    n_ctx = ctx.shape[1]
    assert d == D_MODEL and n_ctx == TOKEN_TILE and n_lat % TOKEN_TILE == 0 and b < MOD_ROWS

    mod = _modulation(c, c_ctx, ada_w, ada_b)
    cos64, sin64 = _rope_tables(n_lat, n_ctx, HEAD_DIM)
    cos32, sin32 = _rope_tables(n_lat, n_ctx, MLA_ROPE)
    xall = jnp.concatenate([ctx, x], axis=1)

    for layer in range(DEPTH):
        last = layer == DEPTH - 1
        tile_off = 1 if last else 0
        i = layer // 2
        g_mix = norm_mix[layer][None, :]
        if layer % 2 == 0:
            lam_init = _lambda_init(layer)
            ones = jnp.ones((HEAD_DIM,), F32)
            gains = _col(jnp.concatenate([
                jnp.tile(diff_qk_norm[i, 0] * QK_SCALE_64, 8), jnp.tile(diff_qk_norm[i, 1], 8),
                jnp.tile(ones, 8),
                jnp.tile(gqa_qk_norm[i, 0] * QK_SCALE_64, 8), jnp.tile(gqa_qk_norm[i, 1], 2),
                jnp.tile(ones, 2)]))
            q_t, k_t, v_t = _inproj_even(xall, mod, layer, g_mix, ab_w_in[i].T.astype(BF16), gains,
                                         cos64, sin64)
            o_t = _attention(
                functools.partial(_attn_even_body, lam_init=lam_init), q_t, k_t, v_t,
                [diff_lambda[i], _col(diff_subln[i] * (1.0 - lam_init))],
                [_const_spec((4, HEAD_DIM)), _const_spec((DIFF_V_DIM, 1))],
                D_MODEL, tile_off, "attention_even")
            w_out = ab_w_out[i]
        else:
            q_t, k_t, v_t = _inproj_odd(
                xall, mod, layer, g_mix, mla_w_in[i].T.astype(BF16), _col(mla_q_norm[i]),
                mla_w_q_up[i].T.astype(BF16), _col(mla_kv_norm[i]), mla_w_kv_up[i].T.astype(BF16),
                _col(mla_qk_norm[i, 0] * MLA_SCALE), _col(mla_qk_norm[i, 1]), cos32, sin32)
            o_t = _attention(_attn_odd_body, q_t, k_t, v_t, [], [], D_MODEL, tile_off, "attention_odd")
            w_out = mla_w_out[i]
        xall = _out_mlp(xall, o_t, mod, layer, norm_mlp[layer][None, :], w_out.astype(BF16),
                        mlp_w1[layer].astype(BF16), mlp_w2[layer].astype(BF16), tile_off)
    return xall
```

```python
import functools
import math

import jax
import jax.numpy as jnp
from jax import lax
from jax.experimental import pallas as pl
from jax.experimental.pallas import tpu as pltpu

F32 = jnp.float32
BF16 = jnp.bfloat16

D_MODEL = 1024
DEPTH = 4
GRID_W = 64
D_FF = 4 * D_MODEL
N_MOD = 6
ROPE_THETA = 10000.0
RMS_EPS = 1e-6
LOG2E = math.log2(math.e)

HEAD_DIM = 64
DIFF_HEADS = 4
DIFF_V_DIM = 128
GQA_Q_HEADS = 8
GQA_KV_HEADS = 2
GQA_GROUP = GQA_Q_HEADS // GQA_KV_HEADS
AB_IN_W = 2304
QK_SCALE_64 = HEAD_DIM ** -0.5

MLA_NOPE = 64
MLA_ROPE = 32
MLA_QK_DIM = MLA_NOPE + MLA_ROPE
MLA_V = 64
MLA_HEADS = 16
MLA_Q_RANK = 512
MLA_KV_RANK = 256
MLA_IN_W = MLA_Q_RANK + MLA_KV_RANK + MLA_ROPE
MLA_SCALE = MLA_QK_DIM ** -0.5

BF16_ROWS = 16
ONES_ROWS = BF16_ROWS
EVEN_V_BLOCK = ONES_ROWS + DIFF_V_DIM
EVEN_V_ROWS = (DIFF_HEADS + GQA_KV_HEADS) * EVEN_V_BLOCK
ODD_V_BLOCK = ONES_ROWS + MLA_V
ODD_V_ROWS = MLA_HEADS * ODD_V_BLOCK
N_UNITS = 16

TOKEN_TILE = 256
QUERY_TILE = 512
MOD_ROWS = 40
FF_CHUNK = 1024
VMEM_LIMIT_BYTES = 56 * 1024 * 1024

_NT = (((1,), (1,)), ((), ()))
_TN = (((0,), (0,)), ((), ()))


def _lambda_init(layer):
    return 0.8 - 0.6 * math.exp(-0.3 * layer)


def _params(n_axes):
    return pltpu.CompilerParams(
        dimension_semantics=("arbitrary",) * n_axes, vmem_limit_bytes=VMEM_LIMIT_BYTES)


def _const_spec(shape):
    return pl.BlockSpec(shape, lambda *_: (0,) * len(shape))


def _rms_modulate(x, gain, shift, scale):
    ms = jnp.mean(x * x, axis=-1, keepdims=True)
    return (x * lax.rsqrt(ms + RMS_EPS) * gain) * (1.0 + scale) + shift


def _rms_rows(y, n):
    return lax.rsqrt(jnp.sum(y * y, axis=0, keepdims=True) * (1.0 / n) + RMS_EPS)


def _rope_rows(x1, x2, cos, sin):
    return x1 * cos - x2 * sin, x2 * cos + x1 * sin


def _ones_block(tokens):
    row = lax.broadcasted_iota(jnp.int32, (ONES_ROWS, tokens), 0)
    return (row == 0).astype(BF16)


def _mod_kernel(a_ref, w_ref, b_ref, o_ref):
    a = a_ref[...]
    a = a / (1.0 + jnp.exp(-a))
    o_ref[0] = jnp.dot(a.astype(BF16), w_ref[0].astype(BF16), preferred_element_type=F32) + b_ref[0]


def _modulation(c, c_ctx, ada_w, ada_b):
    b = c.shape[0]
    rows = jnp.concatenate([c, c_ctx[None, :], jnp.zeros((MOD_ROWS - b - 1, D_MODEL), F32)], axis=0)
    tn = 1024
    out = pl.pallas_call(
        _mod_kernel,
        grid=(DEPTH, N_MOD * D_MODEL // tn),
        in_specs=[
            pl.BlockSpec((MOD_ROWS, D_MODEL), lambda l, j: (0, 0)),
            pl.BlockSpec((1, D_MODEL, tn), lambda l, j: (l, 0, j)),
            pl.BlockSpec((1, 1, tn), lambda l, j: (l, 0, j)),
        ],
        out_specs=pl.BlockSpec((1, MOD_ROWS, tn), lambda l, j: (l, 0, j)),
        out_shape=jax.ShapeDtypeStruct((DEPTH, MOD_ROWS, N_MOD * D_MODEL), F32),
        compiler_params=_params(2),
        name="adaln_modulation",
    )(rows, ada_w, ada_b[:, None, :])
    return out.reshape(DEPTH, MOD_ROWS, N_MOD, D_MODEL)


def _mod_spec(layer, ctx_row):
    if ctx_row is None:
        return pl.BlockSpec((None, None, N_MOD, D_MODEL), lambda b, t: (layer, b, 0, 0))
    return pl.BlockSpec((None, None, N_MOD, D_MODEL), lambda b, t: (layer, ctx_row, 0, 0))


def _inproj_even_kernel(x_ref, mod_ref, g_ref, w_ref, gain_ref, cos_ref, sin_ref, q_ref, k_ref, v_ref):
    h = _rms_modulate(x_ref[0], g_ref[...], mod_ref[0:1, :], mod_ref[1:2, :]).astype(BF16)
    cos = cos_ref[...]
    sin = sin_ref[...]
    tokens = h.shape[0]

    def proj(r0, r1):
        return lax.dot_general(w_ref[r0:r1, :], h, _NT, preferred_element_type=F32)

    def qk_heads(r0, n_heads, dst_ref, d0):
        y = proj(r0, r0 + n_heads * HEAD_DIM)
        for i in range(n_heads):
            yh = y[i * HEAD_DIM:(i + 1) * HEAD_DIM]
            gcol = gain_ref[r0 + i * HEAD_DIM:r0 + (i + 1) * HEAD_DIM, :]
            yn = yh * _rms_rows(yh, HEAD_DIM) * gcol
            o1, o2 = _rope_rows(yn[0:32], yn[32:64], cos, sin)
            base = d0 + i * HEAD_DIM
            dst_ref[0, base:base + 32, :] = o1.astype(BF16)
            dst_ref[0, base + 32:base + 64, :] = o2.astype(BF16)

    qk_heads(0, 8, q_ref, 0)
    qk_heads(512, 8, k_ref, 0)
    qk_heads(1536, 8, q_ref, 512)
    qk_heads(2048, 2, k_ref, 512)
    ones = _ones_block(tokens)
    av = proj(1024, 1536).astype(BF16)
    for i in range(DIFF_HEADS):
        r = i * EVEN_V_BLOCK
        v_ref[0, r:r + ONES_ROWS, :] = ones
        v_ref[0, r + ONES_ROWS:r + EVEN_V_BLOCK, :] = av[i * DIFF_V_DIM:(i + 1) * DIFF_V_DIM]
    bv = proj(2176, 2304).astype(BF16)
    for j in range(GQA_KV_HEADS):
        r = (DIFF_HEADS + j) * EVEN_V_BLOCK
        v_ref[0, r:r + ONES_ROWS, :] = ones
        v_ref[0, r + ONES_ROWS:r + ONES_ROWS + HEAD_DIM, :] = bv[j * HEAD_DIM:(j + 1) * HEAD_DIM]
        v_ref[0, r + ONES_ROWS + HEAD_DIM:r + EVEN_V_BLOCK, :] = jnp.zeros(
            (EVEN_V_BLOCK - ONES_ROWS - HEAD_DIM, tokens), BF16)


def _inproj_even(x, mod, layer, ctx_row, g, w_t, gains, cos_t, sin_t):
    b, n, _ = x.shape
    tok = lambda rows: pl.BlockSpec((1, rows, TOKEN_TILE), lambda i, t: (i, 0, t))
    rope = pl.BlockSpec((32, TOKEN_TILE), lambda i, t: (0, t))
    return pl.pallas_call(
        _inproj_even_kernel,
        grid=(b, n // TOKEN_TILE),
        in_specs=[
            pl.BlockSpec((1, TOKEN_TILE, D_MODEL), lambda i, t: (i, t, 0)),
            _mod_spec(layer, ctx_row),
            _const_spec((1, D_MODEL)),
            _const_spec((AB_IN_W, D_MODEL)),
            _const_spec((AB_IN_W, 1)),
            rope, rope,
        ],
        out_specs=[tok(1024), tok(640), tok(EVEN_V_ROWS)],
        out_shape=[jax.ShapeDtypeStruct((b, 1024, n), BF16),
                   jax.ShapeDtypeStruct((b, 640, n), BF16),
                   jax.ShapeDtypeStruct((b, EVEN_V_ROWS, n), BF16)],
        compiler_params=_params(2),
        name="inproj_even",
    )(x, mod, g, w_t, gains, cos_t, sin_t)


def _inproj_odd_kernel(x_ref, mod_ref, g_ref, w_ref, qn_ref, wq_ref, kvn_ref, wkv_ref, qg_ref, kg_ref,
                       cos_ref, sin_ref, q_ref, k_ref, v_ref):
    h = _rms_modulate(x_ref[0], g_ref[...], mod_ref[0:1, :], mod_ref[1:2, :]).astype(BF16)
    cos = cos_ref[...]
    sin = sin_ref[...]
    y = lax.dot_general(w_ref[...], h, _NT, preferred_element_type=F32)
    qc = y[0:MLA_Q_RANK]
    kvc = y[MLA_Q_RANK:MLA_Q_RANK + MLA_KV_RANK]
    kr = y[MLA_Q_RANK + MLA_KV_RANK:MLA_IN_W]
    qcn = (qc * _rms_rows(qc, MLA_Q_RANK) * qn_ref[...]).astype(BF16)
    kvcn = (kvc * _rms_rows(kvc, MLA_KV_RANK) * kvn_ref[...]).astype(BF16)
    q = jnp.dot(wq_ref[...], qcn, preferred_element_type=F32)
    kv = jnp.dot(wkv_ref[...], kvcn, preferred_element_type=F32)
    kr_ss = jnp.sum(kr * kr, axis=0, keepdims=True)
    qg = qg_ref[...]
    kg = kg_ref[...]
    ones = _ones_block(h.shape[0])
    for i in range(MLA_HEADS):
        b0 = i * MLA_QK_DIM
        qh = q[b0:b0 + MLA_QK_DIM]
        qn = qh * _rms_rows(qh, MLA_QK_DIM) * qg
        o1, o2 = _rope_rows(qn[64:80], qn[80:96], cos, sin)
        q_ref[0, b0:b0 + 64, :] = qn[0:64].astype(BF16)
        q_ref[0, b0 + 64:b0 + 80, :] = o1.astype(BF16)
        q_ref[0, b0 + 80:b0 + 96, :] = o2.astype(BF16)
        kn = kv[i * 128:i * 128 + MLA_NOPE]
        r = lax.rsqrt((jnp.sum(kn * kn, axis=0, keepdims=True) + kr_ss) * (1.0 / MLA_QK_DIM) + RMS_EPS)
        krn = kr * r * kg[64:96]
        o1, o2 = _rope_rows(krn[0:16], krn[16:32], cos, sin)
        k_ref[0, b0:b0 + 64, :] = (kn * r * kg[0:64]).astype(BF16)
        k_ref[0, b0 + 64:b0 + 80, :] = o1.astype(BF16)
        k_ref[0, b0 + 80:b0 + 96, :] = o2.astype(BF16)
        v0 = i * ODD_V_BLOCK
        v_ref[0, v0:v0 + ONES_ROWS, :] = ones
        v_ref[0, v0 + ONES_ROWS:v0 + ODD_V_BLOCK, :] = kv[i * 128 + MLA_NOPE:(i + 1) * 128].astype(BF16)


def _inproj_odd(x, mod, layer, ctx_row, g, w_t, qn, wq_t, kvn, wkv_t, qg, kg, cos_t, sin_t):
    b, n, _ = x.shape
    tok = lambda rows: pl.BlockSpec((1, rows, TOKEN_TILE), lambda i, t: (i, 0, t))
    rope = pl.BlockSpec((16, TOKEN_TILE), lambda i, t: (0, t))
    nq = MLA_HEADS * MLA_QK_DIM
    return pl.pallas_call(
        _inproj_odd_kernel,
        grid=(b, n // TOKEN_TILE),
        in_specs=[
            pl.BlockSpec((1, TOKEN_TILE, D_MODEL), lambda i, t: (i, t, 0)),
            _mod_spec(layer, ctx_row),
            _const_spec((1, D_MODEL)),
            _const_spec((MLA_IN_W, D_MODEL)),
            _const_spec((MLA_Q_RANK, 1)),
            _const_spec((nq, MLA_Q_RANK)),
            _const_spec((MLA_KV_RANK, 1)),
            _const_spec((MLA_HEADS * (MLA_NOPE + MLA_V), MLA_KV_RANK)),
            _const_spec((MLA_QK_DIM, 1)),
            _const_spec((MLA_QK_DIM, 1)),
            rope, rope,
        ],
        out_specs=[tok(nq), tok(nq), tok(ODD_V_ROWS)],
        out_shape=[jax.ShapeDtypeStruct((b, nq, n), BF16),
                   jax.ShapeDtypeStruct((b, nq, n), BF16),
                   jax.ShapeDtypeStruct((b, ODD_V_ROWS, n), BF16)],
        compiler_params=_params(2),
        name="inproj_odd",
    )(x, mod, g, w_t, qn, wq_t, kvn, wkv_t, qg, kg, cos_t, sin_t)


def _attention_kernel(cfg, n_kv, *refs):
    d, v_rows, dv, q_row, k_row, v_row, finish = cfg
    q_ref = refs[0]
    k_refs = refs[1:1 + n_kv]
    v_refs = refs[1 + n_kv:1 + 2 * n_kv]
    *extra, o_ref, s0, s1, m0, m1, stage = refs[1 + 2 * n_kv:]
    s_bufs, m_bufs = (s0, s1), (m0, m1)
    bounds = [0]
    for k_ref in k_refs:
        bounds.append(bounds[-1] + k_ref.shape[2])

    def scores(i, par):
        q = q_ref[0, pl.ds(pl.multiple_of(q_row(i), BF16_ROWS), d), :]
        kr = pl.multiple_of(k_row(i), BF16_ROWS)
        m = None
        for j, k_ref in enumerate(k_refs):
            s = lax.dot_general(k_ref[0, pl.ds(kr, d), :], q, _TN, preferred_element_type=F32)
            s_bufs[par][bounds[j]:bounds[j + 1], :] = s
            mj = jnp.max(s, axis=0, keepdims=True)
            m = mj if m is None else jnp.maximum(m, mj)
        m_bufs[par][...] = jnp.broadcast_to(m, m_bufs[par].shape)

    def values(i, par):
        m = m_bufs[par][0:1, :]
        vr = pl.multiple_of(v_row(i), BF16_ROWS)
        ox = None
        for j, v_ref in enumerate(v_refs):
            p = jnp.exp2(s_bufs[par][bounds[j]:bounds[j + 1], :] - m).astype(BF16)
            oj = jnp.dot(v_ref[0, pl.ds(vr, v_rows), :], p, preferred_element_type=F32)
            ox = oj if ox is None else ox + oj
        stage[i] = ox[ONES_ROWS:ONES_ROWS + dv] * (1.0 / ox[0:1])

    scores(0, 0)

    def step(t, carry):
        @pl.when(t % 2 == 0)
        def _():
            scores(t + 1, 1)
            values(t, 0)

        @pl.when(t % 2 == 1)
        def _():
            scores(t + 1, 0)
            values(t, 1)
        return carry

    lax.fori_loop(0, N_UNITS - 1, step, 0)
    values(N_UNITS - 1, (N_UNITS - 1) % 2)
    finish(stage, extra, o_ref)


def _attention(cfg, q_t, k_ts, v_ts, extras, extra_specs, tq, name):
    b, rq, nq = q_t.shape
    dv = cfg[2]
    nk = sum(k.shape[2] for k in k_ts)
    tile = lambda rows: pl.BlockSpec((1, rows, tq), lambda i, t: (i, 0, t))
    full = lambda a: pl.BlockSpec((1,) + a.shape[1:], lambda i, t: (i, 0, 0))
    return pl.pallas_call(
        functools.partial(_attention_kernel, cfg, len(k_ts)),
        grid=(b, nq // tq),
        in_specs=[tile(rq)] + [full(a) for a in k_ts] + [full(a) for a in v_ts] + extra_specs,
        out_specs=tile(D_MODEL),
        out_shape=jax.ShapeDtypeStruct((b, D_MODEL, nq), BF16),
        scratch_shapes=[pltpu.VMEM((nk, tq), F32), pltpu.VMEM((nk, tq), F32),
                        pltpu.VMEM((8, tq), F32), pltpu.VMEM((8, tq), F32),
                        pltpu.VMEM((N_UNITS, dv, tq), F32)],
        compiler_params=_params(2),
        name=name,
    )(q_t, *k_ts, *v_ts, *extras)


def _even_attention_cfg(lam_init):
    def k_row(i):
        return jnp.where(i < 8, i * HEAD_DIM, 512 + ((i - 8) // GQA_GROUP) * HEAD_DIM)

    def v_row(i):
        return jnp.where(i < 8, i // 2, DIFF_HEADS + (i - 8) // GQA_GROUP) * EVEN_V_BLOCK

    def finish(stage, extra, o_ref):
        dl_ref, sub_ref = extra
        dl = dl_ref[...]
        lam = (jnp.exp(jnp.sum(dl[0:1] * dl[1:2], axis=1, keepdims=True))
               - jnp.exp(jnp.sum(dl[2:3] * dl[3:4], axis=1, keepdims=True)) + lam_init)
        for h in range(DIFF_HEADS):
            o = stage[2 * h] - lam * stage[2 * h + 1]
            o = o * _rms_rows(o, DIFF_V_DIM) * sub_ref[...]
            o_ref[0, h * DIFF_V_DIM:(h + 1) * DIFF_V_DIM, :] = o.astype(BF16)
        for g in range(GQA_Q_HEADS):
            r = 512 + g * HEAD_DIM
            o_ref[0, r:r + HEAD_DIM, :] = stage[8 + g, 0:HEAD_DIM, :].astype(BF16)

    return (HEAD_DIM, EVEN_V_BLOCK, DIFF_V_DIM, lambda i: i * HEAD_DIM, k_row, v_row, finish)


def _odd_attention_cfg():
    def finish(stage, extra, o_ref):
        for i in range(MLA_HEADS):
            o_ref[0, i * MLA_V:(i + 1) * MLA_V, :] = stage[i].astype(BF16)

    return (MLA_QK_DIM, ODD_V_BLOCK, MLA_V, lambda i: i * MLA_QK_DIM, lambda i: i * MLA_QK_DIM,
            lambda i: i * ODD_V_BLOCK, finish)


def _out_mlp_kernel(x_ref, o_ref, mod_ref, g_ref, wo_ref, w1_ref, w2_ref, out_ref):
    y = lax.dot_general(o_ref[0], wo_ref[...], _TN, preferred_element_type=F32)
    x1 = x_ref[0] + mod_ref[2:3, :] * y
    h = _rms_modulate(x1, g_ref[...], mod_ref[3:4, :], mod_ref[4:5, :]).astype(BF16)
    acc = jnp.zeros(x1.shape, F32)
    for c in range(D_FF // FF_CHUNK):
        u = jnp.dot(h, w1_ref[:, c * FF_CHUNK:(c + 1) * FF_CHUNK], preferred_element_type=F32)
        a = jnp.square(jnp.maximum(u, 0.0)).astype(BF16)
        acc = acc + jnp.dot(a, w2_ref[c * FF_CHUNK:(c + 1) * FF_CHUNK, :], preferred_element_type=F32)
    out_ref[0] = x1 + mod_ref[5:6, :] * acc


def _out_mlp(x, o_t, mod, layer, ctx_row, g, wo, w1, w2):
    b, n, _ = x.shape
    return pl.pallas_call(
        _out_mlp_kernel,
        grid=(b, n // TOKEN_TILE),
        in_specs=[
            pl.BlockSpec((1, TOKEN_TILE, D_MODEL), lambda i, t: (i, t, 0)),
            pl.BlockSpec((1, D_MODEL, TOKEN_TILE), lambda i, t: (i, 0, t)),
            _mod_spec(layer, ctx_row),
            _const_spec((1, D_MODEL)),
            _const_spec((D_MODEL, D_MODEL)),
            _const_spec((D_MODEL, D_FF)),
            _const_spec((D_FF, D_MODEL)),
        ],
        out_specs=pl.BlockSpec((1, TOKEN_TILE, D_MODEL), lambda i, t: (i, t, 0)),
        out_shape=jax.ShapeDtypeStruct((b, n, D_MODEL), F32),
        compiler_params=_params(2),
        name="outproj_mlp",
    )(x, o_t, mod, g, wo, w1, w2)


def _rope_tables(n_lat, rot_dim):
    rows = n_lat // GRID_W
    n_freq = rot_dim // 4
    inv_freq = ROPE_THETA ** (-jnp.arange(n_freq, dtype=F32) / n_freq)
    row = jnp.repeat(jnp.arange(rows, dtype=F32), GRID_W)
    col = jnp.tile(jnp.arange(GRID_W, dtype=F32), rows)
    ang = jnp.concatenate([row[:, None] * inv_freq, col[:, None] * inv_freq], axis=-1)
    return jnp.cos(ang).T, jnp.sin(ang).T


def _identity_rope(n_tokens, rot_dim):
    return jnp.ones((rot_dim // 2, n_tokens), F32), jnp.zeros((rot_dim // 2, n_tokens), F32)


def _col(v):
    return v.astype(F32)[:, None]


def kernel(x, c, ctx, c_ctx, ada_w, ada_b, norm_mix, norm_mlp, mlp_w1, mlp_w2, ab_w_in, ab_w_out,
           diff_qk_norm, diff_lambda, diff_subln, gqa_qk_norm, mla_w_in, mla_q_norm, mla_w_q_up,
           mla_kv_norm, mla_w_kv_up, mla_qk_norm, mla_w_out):
    b, n_lat, d = x.shape
    n_ctx = ctx.shape[1]
    assert d == D_MODEL and b < MOD_ROWS
    assert n_lat % QUERY_TILE == 0 and n_ctx % TOKEN_TILE == 0

    mod = _modulation(c, c_ctx, ada_w, ada_b)
    rope_lat = {HEAD_DIM: _rope_tables(n_lat, HEAD_DIM), MLA_ROPE: _rope_tables(n_lat, MLA_ROPE)}
    rope_ctx = {HEAD_DIM: _identity_rope(n_ctx, HEAD_DIM), MLA_ROPE: _identity_rope(n_ctx, MLA_ROPE)}
    streams = {"lat": (None, rope_lat), "ctx": (b, rope_ctx)}
    xs = {"lat": x, "ctx": ctx}

    for layer in range(DEPTH):
        update_ctx = layer < DEPTH - 1
        i = layer // 2
        g_mix = norm_mix[layer][None, :]
        qkv = {}
        if layer % 2 == 0:
            lam_init = _lambda_init(layer)
            ones = jnp.ones((HEAD_DIM,), F32)
            q_scale = QK_SCALE_64 * LOG2E
            gains = _col(jnp.concatenate([
                jnp.tile(diff_qk_norm[i, 0] * q_scale, 8), jnp.tile(diff_qk_norm[i, 1], 8),
                jnp.tile(ones, 8),
                jnp.tile(gqa_qk_norm[i, 0] * q_scale, 8), jnp.tile(gqa_qk_norm[i, 1], 2),
                jnp.tile(ones, 2)]))
            w_t = ab_w_in[i].T.astype(BF16)
            for name, (ctx_row, rope) in streams.items():
                qkv[name] = _inproj_even(xs[name], mod, layer, ctx_row, g_mix, w_t, gains, *rope[HEAD_DIM])
            cfg = _even_attention_cfg(lam_init)
            extras = [diff_lambda[i], _col(diff_subln[i] * (1.0 - lam_init))]
            extra_specs = [_const_spec((4, HEAD_DIM)), _const_spec((DIFF_V_DIM, 1))]
            w_out = ab_w_out[i]
        else:
            weights = (mla_w_in[i].T.astype(BF16), _col(mla_q_norm[i]), mla_w_q_up[i].T.astype(BF16),
                       _col(mla_kv_norm[i]), mla_w_kv_up[i].T.astype(BF16),
                       _col(mla_qk_norm[i, 0] * (MLA_SCALE * LOG2E)), _col(mla_qk_norm[i, 1]))
            for name, (ctx_row, rope) in streams.items():
                qkv[name] = _inproj_odd(xs[name], mod, layer, ctx_row, g_mix, *weights, *rope[MLA_ROPE])
            cfg = _odd_attention_cfg()
            extras, extra_specs = [], []
            w_out = mla_w_out[i]

        o = {"lat": _attention(cfg, qkv["lat"][0], [qkv["lat"][1], qkv["ctx"][1]],
                               [qkv["lat"][2], qkv["ctx"][2]], extras, extra_specs, QUERY_TILE,
                               "attention_lat")}
        if update_ctx:
            o["ctx"] = _attention(cfg, qkv["ctx"][0], [qkv["ctx"][1]], [qkv["ctx"][2]], extras,
                                  extra_specs, TOKEN_TILE, "attention_ctx")
        wo, w1, w2 = w_out.astype(BF16), mlp_w1[layer].astype(BF16), mlp_w2[layer].astype(BF16)
        for name in o:
            xs[name] = _out_mlp(xs[name], o[name], mod, layer, streams[name][0],
                                norm_mlp[layer][None, :], wo, w1, w2)
    return xs["lat"]
```

```python
import functools
import math

import jax
import jax.numpy as jnp
from jax import lax
from jax.experimental import pallas as pl
from jax.experimental.pallas import tpu as pltpu

F32 = jnp.float32
BF16 = jnp.bfloat16

D_MODEL = 1024
DEPTH = 4
GRID_W = 64
D_FF = 4 * D_MODEL
N_MOD = 6
ROPE_THETA = 10000.0
RMS_EPS = 1e-6
LOG2E = math.log2(math.e)

HEAD_DIM = 64
DIFF_HEADS = 4
DIFF_V_DIM = 128
GQA_Q_HEADS = 8
GQA_KV_HEADS = 2
GQA_GROUP = GQA_Q_HEADS // GQA_KV_HEADS
AB_IN_W = 2304
QK_SCALE_64 = HEAD_DIM ** -0.5

MLA_NOPE = 64
MLA_ROPE = 32
MLA_QK_DIM = MLA_NOPE + MLA_ROPE
MLA_V = 64
MLA_HEADS = 16
MLA_Q_RANK = 512
MLA_KV_RANK = 256
MLA_IN_W = MLA_Q_RANK + MLA_KV_RANK + MLA_ROPE
MLA_SCALE = MLA_QK_DIM ** -0.5

BF16_ROWS = 16
ONES_ROWS = BF16_ROWS
EVEN_V_BLOCK = ONES_ROWS + DIFF_V_DIM
EVEN_V_ROWS = (DIFF_HEADS + GQA_KV_HEADS) * EVEN_V_BLOCK
ODD_V_BLOCK = ONES_ROWS + MLA_V
ODD_V_ROWS = MLA_HEADS * ODD_V_BLOCK
N_UNITS = 16

TOKEN_TILE = 512
QUERY_TILE = 512
MOD_ROWS = 40
FF_CHUNK = 1024
VMEM_LIMIT_BYTES = 56 * 1024 * 1024

_NT = (((1,), (1,)), ((), ()))
_TN = (((0,), (0,)), ((), ()))


def _lambda_init(layer):
    return 0.8 - 0.6 * math.exp(-0.3 * layer)


def _params(n_axes):
    return pltpu.CompilerParams(
        dimension_semantics=("arbitrary",) * n_axes, vmem_limit_bytes=VMEM_LIMIT_BYTES)


def _const_spec(shape):
    return pl.BlockSpec(shape, lambda *_: (0,) * len(shape), pipeline_mode=pl.Buffered(1))


def _rms_modulate(x, gain, shift, scale):
    ms = jnp.mean(x * x, axis=-1, keepdims=True)
    return (x * lax.rsqrt(ms + RMS_EPS) * gain) * (1.0 + scale) + shift


def _rms_rows(y, n):
    return lax.rsqrt(jnp.sum(y * y, axis=0, keepdims=True) * (1.0 / n) + RMS_EPS)


def _rope_rows(x1, x2, cos, sin):
    return x1 * cos - x2 * sin, x2 * cos + x1 * sin


def _ones_block(tokens):
    row = lax.broadcasted_iota(jnp.int32, (ONES_ROWS, tokens), 0)
    return (row == 0).astype(BF16)


def _mod_kernel(a_ref, w_ref, b_ref, o_ref):
    a = a_ref[...]
    a = a / (1.0 + jnp.exp(-a))
    o_ref[0] = jnp.dot(a.astype(BF16), w_ref[0].astype(BF16), preferred_element_type=F32) + b_ref[0]


def _modulation(c, c_ctx, ada_w, ada_b):
    b = c.shape[0]
    rows = jnp.concatenate([c, c_ctx[None, :], jnp.zeros((MOD_ROWS - b - 1, D_MODEL), F32)], axis=0)
    tn = 1024
    out = pl.pallas_call(
        _mod_kernel,
        grid=(DEPTH, N_MOD * D_MODEL // tn),
        in_specs=[
            pl.BlockSpec((MOD_ROWS, D_MODEL), lambda l, j: (0, 0)),
            pl.BlockSpec((1, D_MODEL, tn), lambda l, j: (l, 0, j)),
            pl.BlockSpec((1, 1, tn), lambda l, j: (l, 0, j)),
        ],
        out_specs=pl.BlockSpec((1, MOD_ROWS, tn), lambda l, j: (l, 0, j)),
        out_shape=jax.ShapeDtypeStruct((DEPTH, MOD_ROWS, N_MOD * D_MODEL), F32),
        compiler_params=_params(2),
        name="adaln_modulation",
    )(rows, ada_w, ada_b[:, None, :])
    return out.reshape(DEPTH, MOD_ROWS, N_MOD, D_MODEL)


def _mod_spec(layer, ctx_row):
    if ctx_row is None:
        return pl.BlockSpec((None, None, N_MOD, D_MODEL), lambda b, t: (layer, b, 0, 0))
    return pl.BlockSpec((None, None, N_MOD, D_MODEL), lambda b, t: (layer, ctx_row, 0, 0))


def _inproj_even_kernel(x_ref, mod_ref, g_ref, w_ref, gain_ref, cos_ref, sin_ref, q_ref, k_ref, v_ref):
    h = _rms_modulate(x_ref[0], g_ref[...], mod_ref[0:1, :], mod_ref[1:2, :]).astype(BF16)
    cos = cos_ref[...]
    sin = sin_ref[...]
    tokens = h.shape[0]

    def proj(r0, r1):
        return lax.dot_general(w_ref[r0:r1, :], h, _NT, preferred_element_type=F32)

    def qk_heads(r0, n_heads, dst_ref, d0):
        y = proj(r0, r0 + n_heads * HEAD_DIM)
        for i in range(n_heads):
            yh = y[i * HEAD_DIM:(i + 1) * HEAD_DIM]
            gcol = gain_ref[r0 + i * HEAD_DIM:r0 + (i + 1) * HEAD_DIM, :]
            yn = yh * _rms_rows(yh, HEAD_DIM) * gcol
            o1, o2 = _rope_rows(yn[0:32], yn[32:64], cos, sin)
            base = d0 + i * HEAD_DIM
            dst_ref[0, base:base + 32, :] = o1.astype(BF16)
            dst_ref[0, base + 32:base + 64, :] = o2.astype(BF16)

    qk_heads(0, 8, q_ref, 0)
    qk_heads(512, 8, k_ref, 0)
    qk_heads(1536, 8, q_ref, 512)
    qk_heads(2048, 2, k_ref, 512)
    ones = _ones_block(tokens)
    av = proj(1024, 1536).astype(BF16)
    for i in range(DIFF_HEADS):
        r = i * EVEN_V_BLOCK
        v_ref[0, r:r + ONES_ROWS, :] = ones
        v_ref[0, r + ONES_ROWS:r + EVEN_V_BLOCK, :] = av[i * DIFF_V_DIM:(i + 1) * DIFF_V_DIM]
    bv = proj(2176, 2304).astype(BF16)
    for j in range(GQA_KV_HEADS):
        r = (DIFF_HEADS + j) * EVEN_V_BLOCK
        v_ref[0, r:r + ONES_ROWS, :] = ones
        v_ref[0, r + ONES_ROWS:r + ONES_ROWS + HEAD_DIM, :] = bv[j * HEAD_DIM:(j + 1) * HEAD_DIM]
        v_ref[0, r + ONES_ROWS + HEAD_DIM:r + EVEN_V_BLOCK, :] = jnp.zeros(
            (EVEN_V_BLOCK - ONES_ROWS - HEAD_DIM, tokens), BF16)


def _inproj_even(x, mod, layer, ctx_row, g, w_t, gains, cos_t, sin_t):
    b, n, _ = x.shape
    tm = min(TOKEN_TILE, n)
    tok = lambda rows: pl.BlockSpec((1, rows, tm), lambda i, t: (i, 0, t))
    rope = pl.BlockSpec((32, tm), lambda i, t: (0, t))
    return pl.pallas_call(
        _inproj_even_kernel,
        grid=(b, n // tm),
        in_specs=[
            pl.BlockSpec((1, tm, D_MODEL), lambda i, t: (i, t, 0)),
            _mod_spec(layer, ctx_row),
            _const_spec((1, D_MODEL)),
            _const_spec((AB_IN_W, D_MODEL)),
            _const_spec((AB_IN_W, 1)),
            rope, rope,
        ],
        out_specs=[tok(1024), tok(640), tok(EVEN_V_ROWS)],
        out_shape=[jax.ShapeDtypeStruct((b, 1024, n), BF16),
                   jax.ShapeDtypeStruct((b, 640, n), BF16),
                   jax.ShapeDtypeStruct((b, EVEN_V_ROWS, n), BF16)],
        compiler_params=_params(2),
        name="inproj_even",
    )(x, mod, g, w_t, gains, cos_t, sin_t)


def _inproj_odd_kernel(x_ref, mod_ref, g_ref, w_ref, qn_ref, wq_ref, kvn_ref, wkv_ref, qg_ref, kg_ref,
                       cos_ref, sin_ref, q_ref, k_ref, v_ref):
    h = _rms_modulate(x_ref[0], g_ref[...], mod_ref[0:1, :], mod_ref[1:2, :]).astype(BF16)
    cos = cos_ref[...]
    sin = sin_ref[...]
    y = lax.dot_general(w_ref[...], h, _NT, preferred_element_type=F32)
    qc = y[0:MLA_Q_RANK]
    kvc = y[MLA_Q_RANK:MLA_Q_RANK + MLA_KV_RANK]
    kr = y[MLA_Q_RANK + MLA_KV_RANK:MLA_IN_W]
    qcn = (qc * _rms_rows(qc, MLA_Q_RANK) * qn_ref[...]).astype(BF16)
    kvcn = (kvc * _rms_rows(kvc, MLA_KV_RANK) * kvn_ref[...]).astype(BF16)
    q = jnp.dot(wq_ref[...], qcn, preferred_element_type=F32)
    kv = jnp.dot(wkv_ref[...], kvcn, preferred_element_type=F32)
    kr_ss = jnp.sum(kr * kr, axis=0, keepdims=True)
    qg = qg_ref[...]
    kg = kg_ref[...]
    ones = _ones_block(h.shape[0])
    for i in range(MLA_HEADS):
        b0 = i * MLA_QK_DIM
        qh = q[b0:b0 + MLA_QK_DIM]
        qn = qh * _rms_rows(qh, MLA_QK_DIM) * qg
        o1, o2 = _rope_rows(qn[64:80], qn[80:96], cos, sin)
        q_ref[0, b0:b0 + 64, :] = qn[0:64].astype(BF16)
        q_ref[0, b0 + 64:b0 + 80, :] = o1.astype(BF16)
        q_ref[0, b0 + 80:b0 + 96, :] = o2.astype(BF16)
        kn = kv[i * 128:i * 128 + MLA_NOPE]
        r = lax.rsqrt((jnp.sum(kn * kn, axis=0, keepdims=True) + kr_ss) * (1.0 / MLA_QK_DIM) + RMS_EPS)
        krn = kr * r * kg[64:96]
        o1, o2 = _rope_rows(krn[0:16], krn[16:32], cos, sin)
        k_ref[0, b0:b0 + 64, :] = (kn * r * kg[0:64]).astype(BF16)
        k_ref[0, b0 + 64:b0 + 80, :] = o1.astype(BF16)
        k_ref[0, b0 + 80:b0 + 96, :] = o2.astype(BF16)
        v0 = i * ODD_V_BLOCK
        v_ref[0, v0:v0 + ONES_ROWS, :] = ones
        v_ref[0, v0 + ONES_ROWS:v0 + ODD_V_BLOCK, :] = kv[i * 128 + MLA_NOPE:(i + 1) * 128].astype(BF16)


def _inproj_odd(x, mod, layer, ctx_row, g, w_t, qn, wq_t, kvn, wkv_t, qg, kg, cos_t, sin_t):
    b, n, _ = x.shape
    tm = min(TOKEN_TILE, n)
    tok = lambda rows: pl.BlockSpec((1, rows, tm), lambda i, t: (i, 0, t))
    rope = pl.BlockSpec((16, tm), lambda i, t: (0, t))
    nq = MLA_HEADS * MLA_QK_DIM
    return pl.pallas_call(
        _inproj_odd_kernel,
        grid=(b, n // tm),
        in_specs=[
            pl.BlockSpec((1, tm, D_MODEL), lambda i, t: (i, t, 0)),
            _mod_spec(layer, ctx_row),
            _const_spec((1, D_MODEL)),
            _const_spec((MLA_IN_W, D_MODEL)),
            _const_spec((MLA_Q_RANK, 1)),
            _const_spec((nq, MLA_Q_RANK)),
            _const_spec((MLA_KV_RANK, 1)),
            _const_spec((MLA_HEADS * (MLA_NOPE + MLA_V), MLA_KV_RANK)),
            _const_spec((MLA_QK_DIM, 1)),
            _const_spec((MLA_QK_DIM, 1)),
            rope, rope,
        ],
        out_specs=[tok(nq), tok(nq), tok(ODD_V_ROWS)],
        out_shape=[jax.ShapeDtypeStruct((b, nq, n), BF16),
                   jax.ShapeDtypeStruct((b, nq, n), BF16),
                   jax.ShapeDtypeStruct((b, ODD_V_ROWS, n), BF16)],
        compiler_params=_params(2),
        name="inproj_odd",
    )(x, mod, g, w_t, qn, wq_t, kvn, wkv_t, qg, kg, cos_t, sin_t)


def _attention_kernel(cfg, n_kv, *refs):
    d, v_rows, dv, q_row, k_row, v_row, finish = cfg
    q_ref = refs[0]
    k_refs = refs[1:1 + n_kv]
    v_refs = refs[1 + n_kv:1 + 2 * n_kv]
    *extra, o_ref, s0, s1, m0, m1, stage = refs[1 + 2 * n_kv:]
    s_bufs, m_bufs = (s0, s1), (m0, m1)
    bounds = [0]
    for k_ref in k_refs:
        bounds.append(bounds[-1] + k_ref.shape[2])

    def scores(i, par):
        q = q_ref[0, pl.ds(pl.multiple_of(q_row(i), BF16_ROWS), d), :]
        kr = pl.multiple_of(k_row(i), BF16_ROWS)
        m = None
        for j, k_ref in enumerate(k_refs):
            s = lax.dot_general(k_ref[0, pl.ds(kr, d), :], q, _TN, preferred_element_type=F32)
            s_bufs[par][bounds[j]:bounds[j + 1], :] = s
            mj = jnp.max(s, axis=0, keepdims=True)
            m = mj if m is None else jnp.maximum(m, mj)
        m_bufs[par][...] = jnp.broadcast_to(m, m_bufs[par].shape)

    def values(i, par):
        m = m_bufs[par][0:1, :]
        vr = pl.multiple_of(v_row(i), BF16_ROWS)
        ox = None
        for j, v_ref in enumerate(v_refs):
            p = jnp.exp2(s_bufs[par][bounds[j]:bounds[j + 1], :] - m).astype(BF16)
            oj = jnp.dot(v_ref[0, pl.ds(vr, v_rows), :], p, preferred_element_type=F32)
            ox = oj if ox is None else ox + oj
        stage[i] = ox[ONES_ROWS:ONES_ROWS + dv] * (1.0 / ox[0:1])

    scores(0, 0)

    def step(t, carry):
        @pl.when(t % 2 == 0)
        def _():
            scores(t + 1, 1)
            values(t, 0)

        @pl.when(t % 2 == 1)
        def _():
            scores(t + 1, 0)
            values(t, 1)
        return carry

    lax.fori_loop(0, N_UNITS - 1, step, 0)
    values(N_UNITS - 1, (N_UNITS - 1) % 2)
    finish(stage, extra, o_ref)


def _attention(cfg, q_t, k_ts, v_ts, extras, extra_specs, tq, name):
    b, rq, nq = q_t.shape
    dv = cfg[2]
    nk = sum(k.shape[2] for k in k_ts)
    tile = lambda rows: pl.BlockSpec((1, rows, tq), lambda i, t: (i, 0, t))
    full = lambda a: pl.BlockSpec((1,) + a.shape[1:], lambda i, t: (i, 0, 0))
    return pl.pallas_call(
        functools.partial(_attention_kernel, cfg, len(k_ts)),
        grid=(b, nq // tq),
        in_specs=[tile(rq)] + [full(a) for a in k_ts] + [full(a) for a in v_ts] + extra_specs,
        out_specs=tile(D_MODEL),
        out_shape=jax.ShapeDtypeStruct((b, D_MODEL, nq), BF16),
        scratch_shapes=[pltpu.VMEM((nk, tq), F32), pltpu.VMEM((nk, tq), F32),
                        pltpu.VMEM((8, tq), F32), pltpu.VMEM((8, tq), F32),
                        pltpu.VMEM((N_UNITS, dv, tq), F32)],
        compiler_params=_params(2),
        name=name,
    )(q_t, *k_ts, *v_ts, *extras)


def _even_attention_cfg(lam_init):
    def k_row(i):
        return jnp.where(i < 8, i * HEAD_DIM, 512 + ((i - 8) // GQA_GROUP) * HEAD_DIM)

    def v_row(i):
        return jnp.where(i < 8, i // 2, DIFF_HEADS + (i - 8) // GQA_GROUP) * EVEN_V_BLOCK

    def finish(stage, extra, o_ref):
        dl_ref, sub_ref = extra
        dl = dl_ref[...]
        lam = (jnp.exp(jnp.sum(dl[0:1] * dl[1:2], axis=1, keepdims=True))
               - jnp.exp(jnp.sum(dl[2:3] * dl[3:4], axis=1, keepdims=True)) + lam_init)
        for h in range(DIFF_HEADS):
            o = stage[2 * h] - lam * stage[2 * h + 1]
            o = o * _rms_rows(o, DIFF_V_DIM) * sub_ref[...]
            o_ref[0, h * DIFF_V_DIM:(h + 1) * DIFF_V_DIM, :] = o.astype(BF16)
        for g in range(GQA_Q_HEADS):
            r = 512 + g * HEAD_DIM
            o_ref[0, r:r + HEAD_DIM, :] = stage[8 + g, 0:HEAD_DIM, :].astype(BF16)

    return (HEAD_DIM, EVEN_V_BLOCK, DIFF_V_DIM, lambda i: i * HEAD_DIM, k_row, v_row, finish)


def _odd_attention_cfg():
    def finish(stage, extra, o_ref):
        for i in range(MLA_HEADS):
            o_ref[0, i * MLA_V:(i + 1) * MLA_V, :] = stage[i].astype(BF16)

    return (MLA_QK_DIM, ODD_V_BLOCK, MLA_V, lambda i: i * MLA_QK_DIM, lambda i: i * MLA_QK_DIM,
            lambda i: i * ODD_V_BLOCK, finish)


def _out_mlp_kernel(x_ref, o_ref, mod_ref, g_ref, wo_ref, w1_ref, w2_ref, out_ref):
    y = lax.dot_general(o_ref[0], wo_ref[...], _TN, preferred_element_type=F32)
    x1 = x_ref[0] + mod_ref[2:3, :] * y
    h = _rms_modulate(x1, g_ref[...], mod_ref[3:4, :], mod_ref[4:5, :]).astype(BF16)
    acc = jnp.zeros(x1.shape, F32)
    for c in range(D_FF // FF_CHUNK):
        u = jnp.dot(h, w1_ref[:, c * FF_CHUNK:(c + 1) * FF_CHUNK], preferred_element_type=F32)
        a = jnp.square(jnp.maximum(u, 0.0)).astype(BF16)
        acc = acc + jnp.dot(a, w2_ref[c * FF_CHUNK:(c + 1) * FF_CHUNK, :], preferred_element_type=F32)
    out_ref[0] = x1 + mod_ref[5:6, :] * acc


def _out_mlp(x, o_t, mod, layer, ctx_row, g, wo, w1, w2):
    b, n, _ = x.shape
    tm = min(TOKEN_TILE, n)
    return pl.pallas_call(
        _out_mlp_kernel,
        grid=(b, n // tm),
        in_specs=[
            pl.BlockSpec((1, tm, D_MODEL), lambda i, t: (i, t, 0)),
            pl.BlockSpec((1, D_MODEL, tm), lambda i, t: (i, 0, t)),
            _mod_spec(layer, ctx_row),
            _const_spec((1, D_MODEL)),
            _const_spec((D_MODEL, D_MODEL)),
            _const_spec((D_MODEL, D_FF)),
            _const_spec((D_FF, D_MODEL)),
        ],
        out_specs=pl.BlockSpec((1, tm, D_MODEL), lambda i, t: (i, t, 0)),
        out_shape=jax.ShapeDtypeStruct((b, n, D_MODEL), F32),
        compiler_params=_params(2),
        name="outproj_mlp",
    )(x, o_t, mod, g, wo, w1, w2)


def _rope_tables(n_lat, rot_dim):
    rows = n_lat // GRID_W
    n_freq = rot_dim // 4
    inv_freq = ROPE_THETA ** (-jnp.arange(n_freq, dtype=F32) / n_freq)
    row = jnp.repeat(jnp.arange(rows, dtype=F32), GRID_W)
    col = jnp.tile(jnp.arange(GRID_W, dtype=F32), rows)
    ang = jnp.concatenate([row[:, None] * inv_freq, col[:, None] * inv_freq], axis=-1)
    return jnp.cos(ang).T, jnp.sin(ang).T


def _identity_rope(n_tokens, rot_dim):
    return jnp.ones((rot_dim // 2, n_tokens), F32), jnp.zeros((rot_dim // 2, n_tokens), F32)


def _col(v):
    return v.astype(F32)[:, None]


def kernel(x, c, ctx, c_ctx, ada_w, ada_b, norm_mix, norm_mlp, mlp_w1, mlp_w2, ab_w_in, ab_w_out,
           diff_qk_norm, diff_lambda, diff_subln, gqa_qk_norm, mla_w_in, mla_q_norm, mla_w_q_up,
           mla_kv_norm, mla_w_kv_up, mla_qk_norm, mla_w_out):
    b, n_lat, d = x.shape
    n_ctx = ctx.shape[1]
    assert d == D_MODEL and b < MOD_ROWS
    assert n_lat % QUERY_TILE == 0 and n_lat % TOKEN_TILE == 0 and n_ctx <= TOKEN_TILE

    mod = _modulation(c, c_ctx, ada_w, ada_b)
    rope_lat = {HEAD_DIM: _rope_tables(n_lat, HEAD_DIM), MLA_ROPE: _rope_tables(n_lat, MLA_ROPE)}
    rope_ctx = {HEAD_DIM: _identity_rope(n_ctx, HEAD_DIM), MLA_ROPE: _identity_rope(n_ctx, MLA_ROPE)}
    streams = {"lat": (None, rope_lat), "ctx": (b, rope_ctx)}
    xs = {"lat": x, "ctx": ctx}

    for layer in range(DEPTH):
        update_ctx = layer < DEPTH - 1
        i = layer // 2
        g_mix = norm_mix[layer][None, :]
        qkv = {}
        if layer % 2 == 0:
            lam_init = _lambda_init(layer)
            ones = jnp.ones((HEAD_DIM,), F32)
            q_scale = QK_SCALE_64 * LOG2E
            gains = _col(jnp.concatenate([
                jnp.tile(diff_qk_norm[i, 0] * q_scale, 8), jnp.tile(diff_qk_norm[i, 1], 8),
                jnp.tile(ones, 8),
                jnp.tile(gqa_qk_norm[i, 0] * q_scale, 8), jnp.tile(gqa_qk_norm[i, 1], 2),
                jnp.tile(ones, 2)]))
            w_t = ab_w_in[i].T.astype(BF16)
            for name, (ctx_row, rope) in streams.items():
                qkv[name] = _inproj_even(xs[name], mod, layer, ctx_row, g_mix, w_t, gains, *rope[HEAD_DIM])
            cfg = _even_attention_cfg(lam_init)
            extras = [diff_lambda[i], _col(diff_subln[i] * (1.0 - lam_init))]
            extra_specs = [_const_spec((4, HEAD_DIM)), _const_spec((DIFF_V_DIM, 1))]
            w_out = ab_w_out[i]
        else:
            weights = (mla_w_in[i].T.astype(BF16), _col(mla_q_norm[i]), mla_w_q_up[i].T.astype(BF16),
                       _col(mla_kv_norm[i]), mla_w_kv_up[i].T.astype(BF16),
                       _col(mla_qk_norm[i, 0] * (MLA_SCALE * LOG2E)), _col(mla_qk_norm[i, 1]))
            for name, (ctx_row, rope) in streams.items():
                qkv[name] = _inproj_odd(xs[name], mod, layer, ctx_row, g_mix, *weights, *rope[MLA_ROPE])
            cfg = _odd_attention_cfg()
            extras, extra_specs = [], []
            w_out = mla_w_out[i]

        o = {"lat": _attention(cfg, qkv["lat"][0], [qkv["lat"][1], qkv["ctx"][1]],
                               [qkv["lat"][2], qkv["ctx"][2]], extras, extra_specs, QUERY_TILE,
                               "attention_lat")}
        if update_ctx:
            o["ctx"] = _attention(cfg, qkv["ctx"][0], [qkv["ctx"][1]], [qkv["ctx"][2]], extras,
                                  extra_specs, n_ctx, "attention_ctx")
        wo, w1, w2 = w_out.astype(BF16), mlp_w1[layer].astype(BF16), mlp_w2[layer].astype(BF16)
        for name in o:
            xs[name] = _out_mlp(xs[name], o[name], mod, layer, streams[name][0],
                                norm_mlp[layer][None, :], wo, w1, w2)
    return xs["lat"]
```

```python
import functools
import math

import jax
import jax.numpy as jnp
from jax import lax
from jax.experimental import pallas as pl
from jax.experimental.pallas import tpu as pltpu

F32 = jnp.float32
BF16 = jnp.bfloat16

D_MODEL = 1024
DEPTH = 4
GRID_W = 64
D_FF = 4 * D_MODEL
N_MOD = 6
ROPE_THETA = 10000.0
RMS_EPS = 1e-6
LOG2E = math.log2(math.e)

HEAD_DIM = 64
DIFF_HEADS = 4
DIFF_V_DIM = 128
GQA_Q_HEADS = 8
GQA_KV_HEADS = 2
GQA_GROUP = GQA_Q_HEADS // GQA_KV_HEADS
AB_IN_W = 2304
QK_SCALE_64 = HEAD_DIM ** -0.5

MLA_NOPE = 64
MLA_ROPE = 32
MLA_QK_DIM = MLA_NOPE + MLA_ROPE
MLA_V = 64
MLA_HEADS = 16
MLA_Q_RANK = 512
MLA_KV_RANK = 256
MLA_IN_W = MLA_Q_RANK + MLA_KV_RANK + MLA_ROPE
MLA_SCALE = MLA_QK_DIM ** -0.5

BF16_ROWS = 16
ONES_ROWS = BF16_ROWS
EVEN_V_BLOCK = ONES_ROWS + DIFF_V_DIM
EVEN_V_ROWS = (DIFF_HEADS + GQA_KV_HEADS) * EVEN_V_BLOCK
ODD_V_BLOCK = ONES_ROWS + MLA_V
ODD_V_ROWS = MLA_HEADS * ODD_V_BLOCK
N_UNITS = 16
UNITS_PER_STEP = 4
MAX_ABS_SCORE = 60.0
NORM_MARGIN = 1.05

TOKEN_TILE = 512
QUERY_TILE = 512
MOD_ROWS = 40
FF_CHUNK = 1024
VMEM_LIMIT_BYTES = 56 * 1024 * 1024

_NT = (((1,), (1,)), ((), ()))
_TN = (((0,), (0,)), ((), ()))


def _lambda_init(layer):
    return 0.8 - 0.6 * math.exp(-0.3 * layer)


def _params(n_axes):
    return pltpu.CompilerParams(
        dimension_semantics=("arbitrary",) * n_axes, vmem_limit_bytes=VMEM_LIMIT_BYTES)


def _const_spec(shape):
    return pl.BlockSpec(shape, lambda *_: (0,) * len(shape), pipeline_mode=pl.Buffered(1))


def _rms_modulate(x, gain, shift, scale):
    ms = jnp.mean(x * x, axis=-1, keepdims=True)
    return (x * lax.rsqrt(ms + RMS_EPS) * gain) * (1.0 + scale) + shift


def _rms_rows(y, n):
    return lax.rsqrt(jnp.sum(y * y, axis=0, keepdims=True) * (1.0 / n) + RMS_EPS)


def _rope_rows(x1, x2, cos, sin):
    return x1 * cos - x2 * sin, x2 * cos + x1 * sin


def _ones_block(tokens):
    row = lax.broadcasted_iota(jnp.int32, (ONES_ROWS, tokens), 0)
    return (row == 0).astype(BF16)


def _mod_kernel(a_ref, w_ref, b_ref, o_ref):
    a = a_ref[...]
    a = a / (1.0 + jnp.exp(-a))
    o_ref[0] = jnp.dot(a.astype(BF16), w_ref[0].astype(BF16), preferred_element_type=F32) + b_ref[0]


def _modulation(c, c_ctx, ada_w, ada_b):
    b = c.shape[0]
    rows = jnp.concatenate([c, c_ctx[None, :], jnp.zeros((MOD_ROWS - b - 1, D_MODEL), F32)], axis=0)
    tn = 1024
    out = pl.pallas_call(
        _mod_kernel,
        grid=(DEPTH, N_MOD * D_MODEL // tn),
        in_specs=[
            pl.BlockSpec((MOD_ROWS, D_MODEL), lambda l, j: (0, 0)),
            pl.BlockSpec((1, D_MODEL, tn), lambda l, j: (l, 0, j)),
            pl.BlockSpec((1, 1, tn), lambda l, j: (l, 0, j)),
        ],
        out_specs=pl.BlockSpec((1, MOD_ROWS, tn), lambda l, j: (l, 0, j)),
        out_shape=jax.ShapeDtypeStruct((DEPTH, MOD_ROWS, N_MOD * D_MODEL), F32),
        compiler_params=_params(2),
        name="adaln_modulation",
    )(rows, ada_w, ada_b[:, None, :])
    return out.reshape(DEPTH, MOD_ROWS, N_MOD, D_MODEL)


def _mod_spec(layer, ctx_row):
    if ctx_row is None:
        return pl.BlockSpec((None, None, N_MOD, D_MODEL), lambda b, t: (layer, b, 0, 0))
    return pl.BlockSpec((None, None, N_MOD, D_MODEL), lambda b, t: (layer, ctx_row, 0, 0))


def _inproj_even_kernel(x_ref, mod_ref, g_ref, w_ref, gain_ref, cos_ref, sin_ref, q_ref, k_ref, v_ref):
    h = _rms_modulate(x_ref[0], g_ref[...], mod_ref[0:1, :], mod_ref[1:2, :]).astype(BF16)
    cos = cos_ref[...]
    sin = sin_ref[...]
    tokens = h.shape[0]

    def proj(r0, r1):
        return lax.dot_general(w_ref[r0:r1, :], h, _NT, preferred_element_type=F32)

    def qk_heads(r0, n_heads, dst_ref, d0):
        y = proj(r0, r0 + n_heads * HEAD_DIM)
        for i in range(n_heads):
            yh = y[i * HEAD_DIM:(i + 1) * HEAD_DIM]
            gcol = gain_ref[r0 + i * HEAD_DIM:r0 + (i + 1) * HEAD_DIM, :]
            yn = yh * _rms_rows(yh, HEAD_DIM) * gcol
            o1, o2 = _rope_rows(yn[0:32], yn[32:64], cos, sin)
            base = d0 + i * HEAD_DIM
            dst_ref[0, base:base + 32, :] = o1.astype(BF16)
            dst_ref[0, base + 32:base + 64, :] = o2.astype(BF16)

    qk_heads(0, 8, q_ref, 0)
    qk_heads(512, 8, k_ref, 0)
    qk_heads(1536, 8, q_ref, 512)
    qk_heads(2048, 2, k_ref, 512)
    ones = _ones_block(tokens)
    av = proj(1024, 1536).astype(BF16)
    for i in range(DIFF_HEADS):
        r = i * EVEN_V_BLOCK
        v_ref[0, r:r + ONES_ROWS, :] = ones
        v_ref[0, r + ONES_ROWS:r + EVEN_V_BLOCK, :] = av[i * DIFF_V_DIM:(i + 1) * DIFF_V_DIM]
    bv = proj(2176, 2304).astype(BF16)
    for j in range(GQA_KV_HEADS):
        r = (DIFF_HEADS + j) * EVEN_V_BLOCK
        v_ref[0, r:r + ONES_ROWS, :] = ones
        v_ref[0, r + ONES_ROWS:r + ONES_ROWS + HEAD_DIM, :] = bv[j * HEAD_DIM:(j + 1) * HEAD_DIM]
        v_ref[0, r + ONES_ROWS + HEAD_DIM:r + EVEN_V_BLOCK, :] = jnp.zeros(
            (EVEN_V_BLOCK - ONES_ROWS - HEAD_DIM, tokens), BF16)


def _inproj_even(x, mod, layer, ctx_row, g, w_t, gains, cos_t, sin_t):
    b, n, _ = x.shape
    tm = min(TOKEN_TILE, n)
    tok = lambda rows: pl.BlockSpec((1, rows, tm), lambda i, t: (i, 0, t))
    rope = pl.BlockSpec((32, tm), lambda i, t: (0, t))
    return pl.pallas_call(
        _inproj_even_kernel,
        grid=(b, n // tm),
        in_specs=[
            pl.BlockSpec((1, tm, D_MODEL), lambda i, t: (i, t, 0)),
            _mod_spec(layer, ctx_row),
            _const_spec((1, D_MODEL)),
            _const_spec((AB_IN_W, D_MODEL)),
            _const_spec((AB_IN_W, 1)),
            rope, rope,
        ],
        out_specs=[tok(1024), tok(640), tok(EVEN_V_ROWS)],
        out_shape=[jax.ShapeDtypeStruct((b, 1024, n), BF16),
                   jax.ShapeDtypeStruct((b, 640, n), BF16),
                   jax.ShapeDtypeStruct((b, EVEN_V_ROWS, n), BF16)],
        compiler_params=_params(2),
        name="inproj_even",
    )(x, mod, g, w_t, gains, cos_t, sin_t)


def _inproj_odd_kernel(x_ref, mod_ref, g_ref, w_ref, qn_ref, wq_ref, kvn_ref, wkv_ref, qg_ref, kg_ref,
                       cos_ref, sin_ref, q_ref, k_ref, v_ref):
    h = _rms_modulate(x_ref[0], g_ref[...], mod_ref[0:1, :], mod_ref[1:2, :]).astype(BF16)
    cos = cos_ref[...]
    sin = sin_ref[...]
    y = lax.dot_general(w_ref[...], h, _NT, preferred_element_type=F32)
    qc = y[0:MLA_Q_RANK]
    kvc = y[MLA_Q_RANK:MLA_Q_RANK + MLA_KV_RANK]
    kr = y[MLA_Q_RANK + MLA_KV_RANK:MLA_IN_W]
    qcn = (qc * _rms_rows(qc, MLA_Q_RANK) * qn_ref[...]).astype(BF16)
    kvcn = (kvc * _rms_rows(kvc, MLA_KV_RANK) * kvn_ref[...]).astype(BF16)
    q = jnp.dot(wq_ref[...], qcn, preferred_element_type=F32)
    kv = jnp.dot(wkv_ref[...], kvcn, preferred_element_type=F32)
    kr_ss = jnp.sum(kr * kr, axis=0, keepdims=True)
    qg = qg_ref[...]
    kg = kg_ref[...]
    ones = _ones_block(h.shape[0])
    for i in range(MLA_HEADS):
        b0 = i * MLA_QK_DIM
        qh = q[b0:b0 + MLA_QK_DIM]
        qn = qh * _rms_rows(qh, MLA_QK_DIM) * qg
        o1, o2 = _rope_rows(qn[64:80], qn[80:96], cos, sin)
        q_ref[0, b0:b0 + 64, :] = qn[0:64].astype(BF16)
        q_ref[0, b0 + 64:b0 + 80, :] = o1.astype(BF16)
        q_ref[0, b0 + 80:b0 + 96, :] = o2.astype(BF16)
        kn = kv[i * 128:i * 128 + MLA_NOPE]
        r = lax.rsqrt((jnp.sum(kn * kn, axis=0, keepdims=True) + kr_ss) * (1.0 / MLA_QK_DIM) + RMS_EPS)
        krn = kr * r * kg[64:96]
        o1, o2 = _rope_rows(krn[0:16], krn[16:32], cos, sin)
        k_ref[0, b0:b0 + 64, :] = (kn * r * kg[0:64]).astype(BF16)
        k_ref[0, b0 + 64:b0 + 80, :] = o1.astype(BF16)
        k_ref[0, b0 + 80:b0 + 96, :] = o2.astype(BF16)
        v0 = i * ODD_V_BLOCK
        v_ref[0, v0:v0 + ONES_ROWS, :] = ones
        v_ref[0, v0 + ONES_ROWS:v0 + ODD_V_BLOCK, :] = kv[i * 128 + MLA_NOPE:(i + 1) * 128].astype(BF16)


def _inproj_odd(x, mod, layer, ctx_row, g, w_t, qn, wq_t, kvn, wkv_t, qg, kg, cos_t, sin_t):
    b, n, _ = x.shape
    tm = min(TOKEN_TILE, n)
    tok = lambda rows: pl.BlockSpec((1, rows, tm), lambda i, t: (i, 0, t))
    rope = pl.BlockSpec((16, tm), lambda i, t: (0, t))
    nq = MLA_HEADS * MLA_QK_DIM
    return pl.pallas_call(
        _inproj_odd_kernel,
        grid=(b, n // tm),
        in_specs=[
            pl.BlockSpec((1, tm, D_MODEL), lambda i, t: (i, t, 0)),
            _mod_spec(layer, ctx_row),
            _const_spec((1, D_MODEL)),
            _const_spec((MLA_IN_W, D_MODEL)),
            _const_spec((MLA_Q_RANK, 1)),
            _const_spec((nq, MLA_Q_RANK)),
            _const_spec((MLA_KV_RANK, 1)),
            _const_spec((MLA_HEADS * (MLA_NOPE + MLA_V), MLA_KV_RANK)),
            _const_spec((MLA_QK_DIM, 1)),
            _const_spec((MLA_QK_DIM, 1)),
            rope, rope,
        ],
        out_specs=[tok(nq), tok(nq), tok(ODD_V_ROWS)],
        out_shape=[jax.ShapeDtypeStruct((b, nq, n), BF16),
                   jax.ShapeDtypeStruct((b, nq, n), BF16),
                   jax.ShapeDtypeStruct((b, ODD_V_ROWS, n), BF16)],
        compiler_params=_params(2),
        name="inproj_odd",
    )(x, mod, g, w_t, qn, wq_t, kvn, wkv_t, qg, kg, cos_t, sin_t)


def _attention_kernel(cfg, n_kv, safe_ref, *refs):
    d, v_rows, dv, q_row, k_row, v_row, finish = cfg
    q_ref = refs[0]
    k_refs = refs[1:1 + n_kv]
    v_refs = refs[1 + n_kv:1 + 2 * n_kv]
    *extra, o_ref, s0, s1, m0, m1, stage = refs[1 + 2 * n_kv:]
    s_bufs, m_bufs = (s0, s1), (m0, m1)
    bounds = [0]
    for k_ref in k_refs:
        bounds.append(bounds[-1] + k_ref.shape[2])

    def q_block(i):
        return q_ref[0, pl.ds(pl.multiple_of(q_row(i), BF16_ROWS), d), :]

    def normalised(ox):
        return ox[ONES_ROWS:ONES_ROWS + dv] * (1.0 / ox[0:1])

    def unit_shift_free(i):
        q = q_block(i)
        kr = pl.multiple_of(k_row(i), BF16_ROWS)
        vr = pl.multiple_of(v_row(i), BF16_ROWS)
        ox = None
        for k_ref, v_ref in zip(k_refs, v_refs):
            s = lax.dot_general(k_ref[0, pl.ds(kr, d), :], q, _TN, preferred_element_type=F32)
            oj = jnp.dot(v_ref[0, pl.ds(vr, v_rows), :], jnp.exp2(s).astype(BF16),
                         preferred_element_type=F32)
            ox = oj if ox is None else ox + oj
        stage[i] = normalised(ox)

    def scores(i, par):
        q = q_block(i)
        kr = pl.multiple_of(k_row(i), BF16_ROWS)
        m = None
        for j, k_ref in enumerate(k_refs):
            s = lax.dot_general(k_ref[0, pl.ds(kr, d), :], q, _TN, preferred_element_type=F32)
            s_bufs[par][bounds[j]:bounds[j + 1], :] = s
            mj = jnp.max(s, axis=0, keepdims=True)
            m = mj if m is None else jnp.maximum(m, mj)
        m_bufs[par][...] = jnp.broadcast_to(m, m_bufs[par].shape)

    def values(i, par):
        m = m_bufs[par][0:1, :]
        vr = pl.multiple_of(v_row(i), BF16_ROWS)
        ox = None
        for j, v_ref in enumerate(v_refs):
            p = jnp.exp2(s_bufs[par][bounds[j]:bounds[j + 1], :] - m).astype(BF16)
            oj = jnp.dot(v_ref[0, pl.ds(vr, v_rows), :], p, preferred_element_type=F32)
            ox = oj if ox is None else ox + oj
        stage[i] = normalised(ox)

    @pl.when(safe_ref[0] != 0)
    def _():
        def step(t, carry):
            for u in range(UNITS_PER_STEP):
                unit_shift_free(t * UNITS_PER_STEP + u)
            return carry

        lax.fori_loop(0, N_UNITS // UNITS_PER_STEP, step, 0)

    @pl.when(safe_ref[0] == 0)
    def _():
        scores(0, 0)

        def step(t, carry):
            @pl.when(t % 2 == 0)
            def _():
                scores(t + 1, 1)
                values(t, 0)

            @pl.when(t % 2 == 1)
            def _():
                scores(t + 1, 0)
                values(t, 1)
            return carry

        lax.fori_loop(0, N_UNITS - 1, step, 0)
        values(N_UNITS - 1, (N_UNITS - 1) % 2)

    finish(stage, extra, o_ref)


def _attention(cfg, safe, q_t, k_ts, v_ts, extras, extra_specs, tq, name):
    b, rq, nq = q_t.shape
    dv = cfg[2]
    nk = sum(k.shape[2] for k in k_ts)
    tile = lambda rows: pl.BlockSpec((1, rows, tq), lambda i, t, safe_ref: (i, 0, t))
    full = lambda a: pl.BlockSpec((1,) + a.shape[1:], lambda i, t, safe_ref: (i, 0, 0))
    return pl.pallas_call(
        functools.partial(_attention_kernel, cfg, len(k_ts)),
        grid_spec=pltpu.PrefetchScalarGridSpec(
            num_scalar_prefetch=1,
            grid=(b, nq // tq),
            in_specs=[tile(rq)] + [full(a) for a in k_ts] + [full(a) for a in v_ts] + extra_specs,
            out_specs=tile(D_MODEL),
            scratch_shapes=[pltpu.VMEM((nk, tq), F32), pltpu.VMEM((nk, tq), F32),
                            pltpu.VMEM((8, tq), F32), pltpu.VMEM((8, tq), F32),
                            pltpu.VMEM((N_UNITS, dv, tq), F32)]),
        out_shape=jax.ShapeDtypeStruct((b, D_MODEL, nq), BF16),
        compiler_params=_params(2),
        name=name,
    )(safe, q_t, *k_ts, *v_ts, *extras)


def _even_attention_cfg(lam_init):
    def k_row(i):
        return jnp.where(i < 8, i * HEAD_DIM, 512 + ((i - 8) // GQA_GROUP) * HEAD_DIM)

    def v_row(i):
        return jnp.where(i < 8, i // 2, DIFF_HEADS + (i - 8) // GQA_GROUP) * EVEN_V_BLOCK

    def finish(stage, extra, o_ref):
        dl_ref, sub_ref = extra
        dl = dl_ref[...]
        lam = (jnp.exp(jnp.sum(dl[0:1] * dl[1:2], axis=1, keepdims=True))
               - jnp.exp(jnp.sum(dl[2:3] * dl[3:4], axis=1, keepdims=True)) + lam_init)
        for h in range(DIFF_HEADS):
            o = stage[2 * h] - lam * stage[2 * h + 1]
            o = o * _rms_rows(o, DIFF_V_DIM) * sub_ref[...]
            o_ref[0, h * DIFF_V_DIM:(h + 1) * DIFF_V_DIM, :] = o.astype(BF16)
        for g in range(GQA_Q_HEADS):
            r = 512 + g * HEAD_DIM
            o_ref[0, r:r + HEAD_DIM, :] = stage[8 + g, 0:HEAD_DIM, :].astype(BF16)

    return (HEAD_DIM, EVEN_V_BLOCK, DIFF_V_DIM, lambda i: i * HEAD_DIM, k_row, v_row, finish)


def _odd_attention_cfg():
    def finish(stage, extra, o_ref):
        for i in range(MLA_HEADS):
            o_ref[0, i * MLA_V:(i + 1) * MLA_V, :] = stage[i].astype(BF16)

    return (MLA_QK_DIM, ODD_V_BLOCK, MLA_V, lambda i: i * MLA_QK_DIM, lambda i: i * MLA_QK_DIM,
            lambda i: i * ODD_V_BLOCK, finish)


def _out_mlp_kernel(x_ref, o_ref, mod_ref, g_ref, wo_ref, w1_ref, w2_ref, out_ref):
    y = lax.dot_general(o_ref[0], wo_ref[...], _TN, preferred_element_type=F32)
    x1 = x_ref[0] + mod_ref[2:3, :] * y
    h = _rms_modulate(x1, g_ref[...], mod_ref[3:4, :], mod_ref[4:5, :]).astype(BF16)
    acc = jnp.zeros(x1.shape, F32)
    for c in range(D_FF // FF_CHUNK):
        u = jnp.dot(h, w1_ref[:, c * FF_CHUNK:(c + 1) * FF_CHUNK], preferred_element_type=F32)
        a = jnp.square(jnp.maximum(u, 0.0)).astype(BF16)
        acc = acc + jnp.dot(a, w2_ref[c * FF_CHUNK:(c + 1) * FF_CHUNK, :], preferred_element_type=F32)
    out_ref[0] = x1 + mod_ref[5:6, :] * acc


def _out_mlp(x, o_t, mod, layer, ctx_row, g, wo, w1, w2):
    b, n, _ = x.shape
    tm = min(TOKEN_TILE, n)
    return pl.pallas_call(
        _out_mlp_kernel,
        grid=(b, n // tm),
        in_specs=[
            pl.BlockSpec((1, tm, D_MODEL), lambda i, t: (i, t, 0)),
            pl.BlockSpec((1, D_MODEL, tm), lambda i, t: (i, 0, t)),
            _mod_spec(layer, ctx_row),
            _const_spec((1, D_MODEL)),
            _const_spec((D_MODEL, D_MODEL)),
            _const_spec((D_MODEL, D_FF)),
            _const_spec((D_FF, D_MODEL)),
        ],
        out_specs=pl.BlockSpec((1, tm, D_MODEL), lambda i, t: (i, t, 0)),
        out_shape=jax.ShapeDtypeStruct((b, n, D_MODEL), F32),
        compiler_params=_params(2),
        name="outproj_mlp",
    )(x, o_t, mod, g, wo, w1, w2)


def _rope_tables(n_lat, rot_dim):
    rows = n_lat // GRID_W
    n_freq = rot_dim // 4
    inv_freq = ROPE_THETA ** (-jnp.arange(n_freq, dtype=F32) / n_freq)
    row = jnp.repeat(jnp.arange(rows, dtype=F32), GRID_W)
    col = jnp.tile(jnp.arange(GRID_W, dtype=F32), rows)
    ang = jnp.concatenate([row[:, None] * inv_freq, col[:, None] * inv_freq], axis=-1)
    return jnp.cos(ang).T, jnp.sin(ang).T


def _identity_rope(n_tokens, rot_dim):
    return jnp.ones((rot_dim // 2, n_tokens), F32), jnp.zeros((rot_dim // 2, n_tokens), F32)


def _col(v):
    return v.astype(F32)[:, None]


def _max_sq(gain):
    return jnp.max(jnp.square(gain.astype(F32)))


def kernel(x, c, ctx, c_ctx, ada_w, ada_b, norm_mix, norm_mlp, mlp_w1, mlp_w2, ab_w_in, ab_w_out,
           diff_qk_norm, diff_lambda, diff_subln, gqa_qk_norm, mla_w_in, mla_q_norm, mla_w_q_up,
           mla_kv_norm, mla_w_kv_up, mla_qk_norm, mla_w_out):
    b, n_lat, d = x.shape
    n_ctx = ctx.shape[1]
    assert d == D_MODEL and b < MOD_ROWS
    assert n_lat % QUERY_TILE == 0 and n_lat % TOKEN_TILE == 0 and n_ctx <= TOKEN_TILE

    mod = _modulation(c, c_ctx, ada_w, ada_b)
    rope_lat = {HEAD_DIM: _rope_tables(n_lat, HEAD_DIM), MLA_ROPE: _rope_tables(n_lat, MLA_ROPE)}
    rope_ctx = {HEAD_DIM: _identity_rope(n_ctx, HEAD_DIM), MLA_ROPE: _identity_rope(n_ctx, MLA_ROPE)}
    streams = {"lat": (None, rope_lat), "ctx": (b, rope_ctx)}
    xs = {"lat": x, "ctx": ctx}

    for layer in range(DEPTH):
        update_ctx = layer < DEPTH - 1
        i = layer // 2
        g_mix = norm_mix[layer][None, :]
        qkv = {}
        if layer % 2 == 0:
            lam_init = _lambda_init(layer)
            ones = jnp.ones((HEAD_DIM,), F32)
            q_scale = QK_SCALE_64 * LOG2E
            gains = _col(jnp.concatenate([
                jnp.tile(diff_qk_norm[i, 0] * q_scale, 8), jnp.tile(diff_qk_norm[i, 1], 8),
                jnp.tile(ones, 8),
                jnp.tile(gqa_qk_norm[i, 0] * q_scale, 8), jnp.tile(gqa_qk_norm[i, 1], 2),
                jnp.tile(ones, 2)]))
            bound2 = HEAD_DIM ** 2 * jnp.maximum(
                _max_sq(diff_qk_norm[i, 0] * q_scale) * _max_sq(diff_qk_norm[i, 1]),
                _max_sq(gqa_qk_norm[i, 0] * q_scale) * _max_sq(gqa_qk_norm[i, 1]))
            w_t = ab_w_in[i].T.astype(BF16)
            for name, (ctx_row, rope) in streams.items():
                qkv[name] = _inproj_even(xs[name], mod, layer, ctx_row, g_mix, w_t, gains, *rope[HEAD_DIM])
            cfg = _even_attention_cfg(lam_init)
            extras = [diff_lambda[i], _col(diff_subln[i] * (1.0 - lam_init))]
            extra_specs = [_const_spec((4, HEAD_DIM)), _const_spec((DIFF_V_DIM, 1))]
            w_out = ab_w_out[i]
        else:
            q_gain = mla_qk_norm[i, 0] * (MLA_SCALE * LOG2E)
            bound2 = MLA_QK_DIM ** 2 * _max_sq(q_gain) * _max_sq(mla_qk_norm[i, 1])
            weights = (mla_w_in[i].T.astype(BF16), _col(mla_q_norm[i]), mla_w_q_up[i].T.astype(BF16),
                       _col(mla_kv_norm[i]), mla_w_kv_up[i].T.astype(BF16),
                       _col(q_gain), _col(mla_qk_norm[i, 1]))
            for name, (ctx_row, rope) in streams.items():
                qkv[name] = _inproj_odd(xs[name], mod, layer, ctx_row, g_mix, *weights, *rope[MLA_ROPE])
            cfg = _odd_attention_cfg()
            extras, extra_specs = [], []
            w_out = mla_w_out[i]

        safe = (bound2 * NORM_MARGIN <= MAX_ABS_SCORE ** 2).astype(jnp.int32).reshape(1)
        o = {"lat": _attention(cfg, safe, qkv["lat"][0], [qkv["lat"][1], qkv["ctx"][1]],
                               [qkv["lat"][2], qkv["ctx"][2]], extras, extra_specs, QUERY_TILE,
                               "attention_lat")}
        if update_ctx:
            o["ctx"] = _attention(cfg, safe, qkv["ctx"][0], [qkv["ctx"][1]], [qkv["ctx"][2]], extras,
                                  extra_specs, n_ctx, "attention_ctx")
        wo, w1, w2 = w_out.astype(BF16), mlp_w1[layer].astype(BF16), mlp_w2[layer].astype(BF16)
        for name in o:
            xs[name] = _out_mlp(xs[name], o[name], mod, layer, streams[name][0],
                                norm_mlp[layer][None, :], wo, w1, w2)
    return xs["lat"]
```

```python
import functools
import math

import jax
import jax.numpy as jnp
from jax import lax
from jax.experimental import pallas as pl
from jax.experimental.pallas import tpu as pltpu

F32 = jnp.float32
BF16 = jnp.bfloat16

D_MODEL = 1024
DEPTH = 4
GRID_W = 64
D_FF = 4 * D_MODEL
N_MOD = 6
ROPE_THETA = 10000.0
RMS_EPS = 1e-6
LOG2E = math.log2(math.e)

HEAD_DIM = 64
DIFF_HEADS = 4
DIFF_V_DIM = 128
GQA_Q_HEADS = 8
GQA_KV_HEADS = 2
GQA_GROUP = GQA_Q_HEADS // GQA_KV_HEADS
AB_IN_W = 2304
QK_SCALE_64 = HEAD_DIM ** -0.5

MLA_NOPE = 64
MLA_ROPE = 32
MLA_QK_DIM = MLA_NOPE + MLA_ROPE
MLA_V = 64
MLA_HEADS = 16
MLA_Q_RANK = 512
MLA_KV_RANK = 256
MLA_IN_W = MLA_Q_RANK + MLA_KV_RANK + MLA_ROPE
MLA_SCALE = MLA_QK_DIM ** -0.5

BF16_ROWS = 16
ONES_ROWS = BF16_ROWS
EVEN_V_BLOCK = ONES_ROWS + DIFF_V_DIM
EVEN_V_ROWS = (DIFF_HEADS + GQA_KV_HEADS) * EVEN_V_BLOCK
ODD_V_BLOCK = ONES_ROWS + MLA_V
ODD_V_ROWS = MLA_HEADS * ODD_V_BLOCK
N_UNITS = 16
UNITS_PER_STEP = 4
FEW_KEYS = 512
MAX_ABS_SCORE = 60.0
NORM_MARGIN = 1.05

TOKEN_TILE = 1024
QUERY_TILE = 512
MOD_ROWS = 40
FF_CHUNK = 1024
VMEM_LIMIT_BYTES = 56 * 1024 * 1024

_NT = (((1,), (1,)), ((), ()))
_TN = (((0,), (0,)), ((), ()))


def _lambda_init(layer):
    return 0.8 - 0.6 * math.exp(-0.3 * layer)


def _params(n_axes):
    return pltpu.CompilerParams(
        dimension_semantics=("arbitrary",) * n_axes, vmem_limit_bytes=VMEM_LIMIT_BYTES)


def _const_spec(shape):
    return pl.BlockSpec(shape, lambda *_: (0,) * len(shape), pipeline_mode=pl.Buffered(1))


def _rms_modulate(x, gain, shift, scale):
    ms = jnp.mean(x * x, axis=-1, keepdims=True)
    return (x * lax.rsqrt(ms + RMS_EPS) * gain) * (1.0 + scale) + shift


def _rms_rows(y, n):
    return lax.rsqrt(jnp.sum(y * y, axis=0, keepdims=True) * (1.0 / n) + RMS_EPS)


def _rope_rows(x1, x2, cos, sin):
    return x1 * cos - x2 * sin, x2 * cos + x1 * sin


def _ones_block(tokens):
    row = lax.broadcasted_iota(jnp.int32, (ONES_ROWS, tokens), 0)
    return (row == 0).astype(BF16)


def _mod_kernel(a_ref, w_ref, b_ref, o_ref):
    a = a_ref[...]
    a = a / (1.0 + jnp.exp(-a))
    o_ref[0] = jnp.dot(a.astype(BF16), w_ref[0].astype(BF16), preferred_element_type=F32) + b_ref[0]


def _modulation(c, c_ctx, ada_w, ada_b):
    b = c.shape[0]
    rows = jnp.concatenate([c, c_ctx[None, :], jnp.zeros((MOD_ROWS - b - 1, D_MODEL), F32)], axis=0)
    tn = 1024
    out = pl.pallas_call(
        _mod_kernel,
        grid=(DEPTH, N_MOD * D_MODEL // tn),
        in_specs=[
            pl.BlockSpec((MOD_ROWS, D_MODEL), lambda l, j: (0, 0)),
            pl.BlockSpec((1, D_MODEL, tn), lambda l, j: (l, 0, j)),
            pl.BlockSpec((1, 1, tn), lambda l, j: (l, 0, j)),
        ],
        out_specs=pl.BlockSpec((1, MOD_ROWS, tn), lambda l, j: (l, 0, j)),
        out_shape=jax.ShapeDtypeStruct((DEPTH, MOD_ROWS, N_MOD * D_MODEL), F32),
        compiler_params=_params(2),
        name="adaln_modulation",
    )(rows, ada_w, ada_b[:, None, :])
    return out.reshape(DEPTH, MOD_ROWS, N_MOD, D_MODEL)


def _mod_spec(layer, ctx_row):
    if ctx_row is None:
        return pl.BlockSpec((None, None, N_MOD, D_MODEL), lambda b, t: (layer, b, 0, 0))
    return pl.BlockSpec((None, None, N_MOD, D_MODEL), lambda b, t: (layer, ctx_row, 0, 0))


def _inproj_even_kernel(x_ref, mod_ref, g_ref, w_ref, gain_ref, cos_ref, sin_ref, q_ref, k_ref, v_ref):
    h = _rms_modulate(x_ref[0], g_ref[...], mod_ref[0:1, :], mod_ref[1:2, :]).astype(BF16)
    cos = cos_ref[...]
    sin = sin_ref[...]
    tokens = h.shape[0]

    def proj(r0, r1):
        return lax.dot_general(w_ref[r0:r1, :], h, _NT, preferred_element_type=F32)

    def qk_heads(r0, n_heads, dst_ref, d0):
        y = proj(r0, r0 + n_heads * HEAD_DIM)
        for i in range(n_heads):
            yh = y[i * HEAD_DIM:(i + 1) * HEAD_DIM]
            gcol = gain_ref[r0 + i * HEAD_DIM:r0 + (i + 1) * HEAD_DIM, :]
            yn = yh * _rms_rows(yh, HEAD_DIM) * gcol
            o1, o2 = _rope_rows(yn[0:32], yn[32:64], cos, sin)
            base = d0 + i * HEAD_DIM
            dst_ref[0, base:base + 32, :] = o1.astype(BF16)
            dst_ref[0, base + 32:base + 64, :] = o2.astype(BF16)

    qk_heads(0, 8, q_ref, 0)
    qk_heads(512, 8, k_ref, 0)
    qk_heads(1536, 8, q_ref, 512)
    qk_heads(2048, 2, k_ref, 512)
    ones = _ones_block(tokens)
    av = proj(1024, 1536).astype(BF16)
    for i in range(DIFF_HEADS):
        r = i * EVEN_V_BLOCK
        v_ref[0, r:r + ONES_ROWS, :] = ones
        v_ref[0, r + ONES_ROWS:r + EVEN_V_BLOCK, :] = av[i * DIFF_V_DIM:(i + 1) * DIFF_V_DIM]
    bv = proj(2176, 2304).astype(BF16)
    for j in range(GQA_KV_HEADS):
        r = (DIFF_HEADS + j) * EVEN_V_BLOCK
        v_ref[0, r:r + ONES_ROWS, :] = ones
        v_ref[0, r + ONES_ROWS:r + ONES_ROWS + HEAD_DIM, :] = bv[j * HEAD_DIM:(j + 1) * HEAD_DIM]
        v_ref[0, r + ONES_ROWS + HEAD_DIM:r + EVEN_V_BLOCK, :] = jnp.zeros(
            (EVEN_V_BLOCK - ONES_ROWS - HEAD_DIM, tokens), BF16)


def _inproj_even(x, mod, layer, ctx_row, g, w_t, gains, cos_t, sin_t):
    b, n, _ = x.shape
    tm = min(TOKEN_TILE, n)
    tok = lambda rows: pl.BlockSpec((1, rows, tm), lambda i, t: (i, 0, t))
    rope = pl.BlockSpec((32, tm), lambda i, t: (0, t))
    return pl.pallas_call(
        _inproj_even_kernel,
        grid=(b, n // tm),
        in_specs=[
            pl.BlockSpec((1, tm, D_MODEL), lambda i, t: (i, t, 0)),
            _mod_spec(layer, ctx_row),
            _const_spec((1, D_MODEL)),
            _const_spec((AB_IN_W, D_MODEL)),
            _const_spec((AB_IN_W, 1)),
            rope, rope,
        ],
        out_specs=[tok(1024), tok(640), tok(EVEN_V_ROWS)],
        out_shape=[jax.ShapeDtypeStruct((b, 1024, n), BF16),
                   jax.ShapeDtypeStruct((b, 640, n), BF16),
                   jax.ShapeDtypeStruct((b, EVEN_V_ROWS, n), BF16)],
        compiler_params=_params(2),
        name="inproj_even",
    )(x, mod, g, w_t, gains, cos_t, sin_t)


def _inproj_odd_kernel(x_ref, mod_ref, g_ref, w_ref, qn_ref, wq_ref, kvn_ref, wkv_ref, qg_ref, kg_ref,
                       cos_ref, sin_ref, q_ref, k_ref, v_ref):
    h = _rms_modulate(x_ref[0], g_ref[...], mod_ref[0:1, :], mod_ref[1:2, :]).astype(BF16)
    cos = cos_ref[...]
    sin = sin_ref[...]
    y = lax.dot_general(w_ref[...], h, _NT, preferred_element_type=F32)
    qc = y[0:MLA_Q_RANK]
    kvc = y[MLA_Q_RANK:MLA_Q_RANK + MLA_KV_RANK]
    kr = y[MLA_Q_RANK + MLA_KV_RANK:MLA_IN_W]
    qcn = (qc * _rms_rows(qc, MLA_Q_RANK) * qn_ref[...]).astype(BF16)
    kvcn = (kvc * _rms_rows(kvc, MLA_KV_RANK) * kvn_ref[...]).astype(BF16)
    q = jnp.dot(wq_ref[...], qcn, preferred_element_type=F32)
    kv = jnp.dot(wkv_ref[...], kvcn, preferred_element_type=F32)
    kr_ss = jnp.sum(kr * kr, axis=0, keepdims=True)
    qg = qg_ref[...]
    kg = kg_ref[...]
    ones = _ones_block(h.shape[0])
    for i in range(MLA_HEADS):
        b0 = i * MLA_QK_DIM
        qh = q[b0:b0 + MLA_QK_DIM]
        qn = qh * _rms_rows(qh, MLA_QK_DIM) * qg
        o1, o2 = _rope_rows(qn[64:80], qn[80:96], cos, sin)
        q_ref[0, b0:b0 + 64, :] = qn[0:64].astype(BF16)
        q_ref[0, b0 + 64:b0 + 80, :] = o1.astype(BF16)
        q_ref[0, b0 + 80:b0 + 96, :] = o2.astype(BF16)
        kn = kv[i * 128:i * 128 + MLA_NOPE]
        r = lax.rsqrt((jnp.sum(kn * kn, axis=0, keepdims=True) + kr_ss) * (1.0 / MLA_QK_DIM) + RMS_EPS)
        krn = kr * r * kg[64:96]
        o1, o2 = _rope_rows(krn[0:16], krn[16:32], cos, sin)
        k_ref[0, b0:b0 + 64, :] = (kn * r * kg[0:64]).astype(BF16)
        k_ref[0, b0 + 64:b0 + 80, :] = o1.astype(BF16)
        k_ref[0, b0 + 80:b0 + 96, :] = o2.astype(BF16)
        v0 = i * ODD_V_BLOCK
        v_ref[0, v0:v0 + ONES_ROWS, :] = ones
        v_ref[0, v0 + ONES_ROWS:v0 + ODD_V_BLOCK, :] = kv[i * 128 + MLA_NOPE:(i + 1) * 128].astype(BF16)


def _inproj_odd(x, mod, layer, ctx_row, g, w_t, qn, wq_t, kvn, wkv_t, qg, kg, cos_t, sin_t):
    b, n, _ = x.shape
    tm = min(TOKEN_TILE, n)
    tok = lambda rows: pl.BlockSpec((1, rows, tm), lambda i, t: (i, 0, t))
    rope = pl.BlockSpec((16, tm), lambda i, t: (0, t))
    nq = MLA_HEADS * MLA_QK_DIM
    return pl.pallas_call(
        _inproj_odd_kernel,
        grid=(b, n // tm),
        in_specs=[
            pl.BlockSpec((1, tm, D_MODEL), lambda i, t: (i, t, 0)),
            _mod_spec(layer, ctx_row),
            _const_spec((1, D_MODEL)),
            _const_spec((MLA_IN_W, D_MODEL)),
            _const_spec((MLA_Q_RANK, 1)),
            _const_spec((nq, MLA_Q_RANK)),
            _const_spec((MLA_KV_RANK, 1)),
            _const_spec((MLA_HEADS * (MLA_NOPE + MLA_V), MLA_KV_RANK)),
            _const_spec((MLA_QK_DIM, 1)),
            _const_spec((MLA_QK_DIM, 1)),
            rope, rope,
        ],
        out_specs=[tok(nq), tok(nq), tok(ODD_V_ROWS)],
        out_shape=[jax.ShapeDtypeStruct((b, nq, n), BF16),
                   jax.ShapeDtypeStruct((b, nq, n), BF16),
                   jax.ShapeDtypeStruct((b, ODD_V_ROWS, n), BF16)],
        compiler_params=_params(2),
        name="inproj_odd",
    )(x, mod, g, w_t, qn, wq_t, kvn, wkv_t, qg, kg, cos_t, sin_t)


def _attention_kernel(cfg, n_kv, safe_ref, *refs):
    d, v_rows, dv, q_row, k_row, v_row, finish = cfg
    q_ref = refs[0]
    k_refs = refs[1:1 + n_kv]
    v_refs = refs[1 + n_kv:1 + 2 * n_kv]
    *extra, o_ref, s0, s1, m0, m1, stage = refs[1 + 2 * n_kv:]
    s_bufs, m_bufs = (s0, s1), (m0, m1)
    bounds = [0]
    for k_ref in k_refs:
        bounds.append(bounds[-1] + k_ref.shape[2])

    def q_block(i):
        return q_ref[0, pl.ds(pl.multiple_of(q_row(i), BF16_ROWS), d), :]

    def normalised(ox):
        return ox[ONES_ROWS:ONES_ROWS + dv] * (1.0 / ox[0:1])

    def unit_shift_free(i):
        q = q_block(i)
        kr = pl.multiple_of(k_row(i), BF16_ROWS)
        vr = pl.multiple_of(v_row(i), BF16_ROWS)
        ox = None
        for k_ref, v_ref in zip(k_refs, v_refs):
            s = lax.dot_general(k_ref[0, pl.ds(kr, d), :], q, _TN, preferred_element_type=F32)
            oj = jnp.dot(v_ref[0, pl.ds(vr, v_rows), :], jnp.exp2(s).astype(BF16),
                         preferred_element_type=F32)
            ox = oj if ox is None else ox + oj
        stage[i] = normalised(ox)

    def scores(i, par):
        q = q_block(i)
        kr = pl.multiple_of(k_row(i), BF16_ROWS)
        m = None
        for j, k_ref in enumerate(k_refs):
            s = lax.dot_general(k_ref[0, pl.ds(kr, d), :], q, _TN, preferred_element_type=F32)
            s_bufs[par][bounds[j]:bounds[j + 1], :] = s
            mj = jnp.max(s, axis=0, keepdims=True)
            m = mj if m is None else jnp.maximum(m, mj)
        m_bufs[par][...] = jnp.broadcast_to(m, m_bufs[par].shape)

    def values(i, par):
        m = m_bufs[par][0:1, :]
        vr = pl.multiple_of(v_row(i), BF16_ROWS)
        ox = None
        for j, v_ref in enumerate(v_refs):
            p = jnp.exp2(s_bufs[par][bounds[j]:bounds[j + 1], :] - m).astype(BF16)
            oj = jnp.dot(v_ref[0, pl.ds(vr, v_rows), :], p, preferred_element_type=F32)
            ox = oj if ox is None else ox + oj
        stage[i] = normalised(ox)

    @pl.when(safe_ref[0] != 0)
    def _():
        per_step = N_UNITS if bounds[-1] <= FEW_KEYS else UNITS_PER_STEP

        def step(t, carry):
            for u in range(per_step):
                unit_shift_free(t * per_step + u)
            return carry

        if per_step == N_UNITS:
            step(0, 0)
        else:
            lax.fori_loop(0, N_UNITS // per_step, step, 0)

    @pl.when(safe_ref[0] == 0)
    def _():
        scores(0, 0)

        def step(t, carry):
            @pl.when(t % 2 == 0)
            def _():
                scores(t + 1, 1)
                values(t, 0)

            @pl.when(t % 2 == 1)
            def _():
                scores(t + 1, 0)
                values(t, 1)
            return carry

        lax.fori_loop(0, N_UNITS - 1, step, 0)
        values(N_UNITS - 1, (N_UNITS - 1) % 2)

    finish(stage, extra, o_ref)


def _attention(cfg, safe, q_t, k_ts, v_ts, extras, extra_specs, tq, name):
    b, rq, nq = q_t.shape
    dv = cfg[2]
    nk = sum(k.shape[2] for k in k_ts)
    tile = lambda rows: pl.BlockSpec((1, rows, tq), lambda i, t, safe_ref: (i, 0, t))
    full = lambda a: pl.BlockSpec((1,) + a.shape[1:], lambda i, t, safe_ref: (i, 0, 0))
    return pl.pallas_call(
        functools.partial(_attention_kernel, cfg, len(k_ts)),
        grid_spec=pltpu.PrefetchScalarGridSpec(
            num_scalar_prefetch=1,
            grid=(b, nq // tq),
            in_specs=[tile(rq)] + [full(a) for a in k_ts] + [full(a) for a in v_ts] + extra_specs,
            out_specs=tile(D_MODEL),
            scratch_shapes=[pltpu.VMEM((nk, tq), F32), pltpu.VMEM((nk, tq), F32),
                            pltpu.VMEM((8, tq), F32), pltpu.VMEM((8, tq), F32),
                            pltpu.VMEM((N_UNITS, dv, tq), F32)]),
        out_shape=jax.ShapeDtypeStruct((b, D_MODEL, nq), BF16),
        compiler_params=_params(2),
        name=name,
    )(safe, q_t, *k_ts, *v_ts, *extras)


def _even_attention_cfg(lam_init):
    def k_row(i):
        return jnp.where(i < 8, i * HEAD_DIM, 512 + ((i - 8) // GQA_GROUP) * HEAD_DIM)

    def v_row(i):
        return jnp.where(i < 8, i // 2, DIFF_HEADS + (i - 8) // GQA_GROUP) * EVEN_V_BLOCK

    def finish(stage, extra, o_ref):
        dl_ref, sub_ref = extra
        dl = dl_ref[...]
        lam = (jnp.exp(jnp.sum(dl[0:1] * dl[1:2], axis=1, keepdims=True))
               - jnp.exp(jnp.sum(dl[2:3] * dl[3:4], axis=1, keepdims=True)) + lam_init)
        for h in range(DIFF_HEADS):
            o = stage[2 * h] - lam * stage[2 * h + 1]
            o = o * _rms_rows(o, DIFF_V_DIM) * sub_ref[...]
            o_ref[0, h * DIFF_V_DIM:(h + 1) * DIFF_V_DIM, :] = o.astype(BF16)
        for g in range(GQA_Q_HEADS):
            r = 512 + g * HEAD_DIM
            o_ref[0, r:r + HEAD_DIM, :] = stage[8 + g, 0:HEAD_DIM, :].astype(BF16)

    return (HEAD_DIM, EVEN_V_BLOCK, DIFF_V_DIM, lambda i: i * HEAD_DIM, k_row, v_row, finish)


def _odd_attention_cfg():
    def finish(stage, extra, o_ref):
        for i in range(MLA_HEADS):
            o_ref[0, i * MLA_V:(i + 1) * MLA_V, :] = stage[i].astype(BF16)

    return (MLA_QK_DIM, ODD_V_BLOCK, MLA_V, lambda i: i * MLA_QK_DIM, lambda i: i * MLA_QK_DIM,
            lambda i: i * ODD_V_BLOCK, finish)


def _out_mlp_kernel(x_ref, o_ref, mod_ref, g_ref, wo_ref, w1_ref, w2_ref, out_ref):
    y = lax.dot_general(o_ref[0], wo_ref[...], _TN, preferred_element_type=F32)
    x1 = x_ref[0] + mod_ref[2:3, :] * y
    h = _rms_modulate(x1, g_ref[...], mod_ref[3:4, :], mod_ref[4:5, :]).astype(BF16)
    acc = jnp.zeros(x1.shape, F32)
    for c in range(D_FF // FF_CHUNK):
        u = jnp.dot(h, w1_ref[:, c * FF_CHUNK:(c + 1) * FF_CHUNK], preferred_element_type=F32)
        a = jnp.square(jnp.maximum(u, 0.0)).astype(BF16)
        acc = acc + jnp.dot(a, w2_ref[c * FF_CHUNK:(c + 1) * FF_CHUNK, :], preferred_element_type=F32)
    out_ref[0] = x1 + mod_ref[5:6, :] * acc


def _out_mlp(x, o_t, mod, layer, ctx_row, g, wo, w1, w2):
    b, n, _ = x.shape
    tm = min(TOKEN_TILE, n)
    return pl.pallas_call(
        _out_mlp_kernel,
        grid=(b, n // tm),
        in_specs=[
            pl.BlockSpec((1, tm, D_MODEL), lambda i, t: (i, t, 0)),
            pl.BlockSpec((1, D_MODEL, tm), lambda i, t: (i, 0, t)),
            _mod_spec(layer, ctx_row),
            _const_spec((1, D_MODEL)),
            _const_spec((D_MODEL, D_MODEL)),
            _const_spec((D_MODEL, D_FF)),
            _const_spec((D_FF, D_MODEL)),
        ],
        out_specs=pl.BlockSpec((1, tm, D_MODEL), lambda i, t: (i, t, 0)),
        out_shape=jax.ShapeDtypeStruct((b, n, D_MODEL), F32),
        compiler_params=_params(2),
        name="outproj_mlp",
    )(x, o_t, mod, g, wo, w1, w2)


def _rope_tables(n_lat, rot_dim):
    rows = n_lat // GRID_W
    n_freq = rot_dim // 4
    inv_freq = ROPE_THETA ** (-jnp.arange(n_freq, dtype=F32) / n_freq)
    row = jnp.repeat(jnp.arange(rows, dtype=F32), GRID_W)
    col = jnp.tile(jnp.arange(GRID_W, dtype=F32), rows)
    ang = jnp.concatenate([row[:, None] * inv_freq, col[:, None] * inv_freq], axis=-1)
    return jnp.cos(ang).T, jnp.sin(ang).T


def _identity_rope(n_tokens, rot_dim):
    return jnp.ones((rot_dim // 2, n_tokens), F32), jnp.zeros((rot_dim // 2, n_tokens), F32)


def _col(v):
    return v.astype(F32)[:, None]


def _max_sq(gain):
    return jnp.max(jnp.square(gain.astype(F32)))


def kernel(x, c, ctx, c_ctx, ada_w, ada_b, norm_mix, norm_mlp, mlp_w1, mlp_w2, ab_w_in, ab_w_out,
           diff_qk_norm, diff_lambda, diff_subln, gqa_qk_norm, mla_w_in, mla_q_norm, mla_w_q_up,
           mla_kv_norm, mla_w_kv_up, mla_qk_norm, mla_w_out):
    b, n_lat, d = x.shape
    n_ctx = ctx.shape[1]
    assert d == D_MODEL and b < MOD_ROWS
    assert n_lat % QUERY_TILE == 0 and n_lat % TOKEN_TILE == 0 and n_ctx <= TOKEN_TILE

    mod = _modulation(c, c_ctx, ada_w, ada_b)
    rope_lat = {HEAD_DIM: _rope_tables(n_lat, HEAD_DIM), MLA_ROPE: _rope_tables(n_lat, MLA_ROPE)}
    rope_ctx = {HEAD_DIM: _identity_rope(n_ctx, HEAD_DIM), MLA_ROPE: _identity_rope(n_ctx, MLA_ROPE)}
    streams = {"lat": (None, rope_lat), "ctx": (b, rope_ctx)}
    xs = {"lat": x, "ctx": ctx}

    for layer in range(DEPTH):
        update_ctx = layer < DEPTH - 1
        i = layer // 2
        g_mix = norm_mix[layer][None, :]
        qkv = {}
        if layer % 2 == 0:
            lam_init = _lambda_init(layer)
            ones = jnp.ones((HEAD_DIM,), F32)
            q_scale = QK_SCALE_64 * LOG2E
            gains = _col(jnp.concatenate([
                jnp.tile(diff_qk_norm[i, 0] * q_scale, 8), jnp.tile(diff_qk_norm[i, 1], 8),
                jnp.tile(ones, 8),
                jnp.tile(gqa_qk_norm[i, 0] * q_scale, 8), jnp.tile(gqa_qk_norm[i, 1], 2),
                jnp.tile(ones, 2)]))
            bound2 = HEAD_DIM ** 2 * jnp.maximum(
                _max_sq(diff_qk_norm[i, 0] * q_scale) * _max_sq(diff_qk_norm[i, 1]),
                _max_sq(gqa_qk_norm[i, 0] * q_scale) * _max_sq(gqa_qk_norm[i, 1]))
            w_t = ab_w_in[i].T.astype(BF16)
            for name, (ctx_row, rope) in streams.items():
                qkv[name] = _inproj_even(xs[name], mod, layer, ctx_row, g_mix, w_t, gains, *rope[HEAD_DIM])
            cfg = _even_attention_cfg(lam_init)
            extras = [diff_lambda[i], _col(diff_subln[i] * (1.0 - lam_init))]
            extra_specs = [_const_spec((4, HEAD_DIM)), _const_spec((DIFF_V_DIM, 1))]
            w_out = ab_w_out[i]
        else:
            q_gain = mla_qk_norm[i, 0] * (MLA_SCALE * LOG2E)
            bound2 = MLA_QK_DIM ** 2 * _max_sq(q_gain) * _max_sq(mla_qk_norm[i, 1])
            weights = (mla_w_in[i].T.astype(BF16), _col(mla_q_norm[i]), mla_w_q_up[i].T.astype(BF16),
                       _col(mla_kv_norm[i]), mla_w_kv_up[i].T.astype(BF16),
                       _col(q_gain), _col(mla_qk_norm[i, 1]))
            for name, (ctx_row, rope) in streams.items():
                qkv[name] = _inproj_odd(xs[name], mod, layer, ctx_row, g_mix, *weights, *rope[MLA_ROPE])
            cfg = _odd_attention_cfg()
            extras, extra_specs = [], []
            w_out = mla_w_out[i]

        safe = (bound2 * NORM_MARGIN <= MAX_ABS_SCORE ** 2).astype(jnp.int32).reshape(1)
        o = {"lat": _attention(cfg, safe, qkv["lat"][0], [qkv["lat"][1], qkv["ctx"][1]],
                               [qkv["lat"][2], qkv["ctx"][2]], extras, extra_specs, QUERY_TILE,
                               "attention_lat")}
        if update_ctx:
            o["ctx"] = _attention(cfg, safe, qkv["ctx"][0], [qkv["ctx"][1]], [qkv["ctx"][2]], extras,
                                  extra_specs, n_ctx, "attention_ctx")
        wo, w1, w2 = w_out.astype(BF16), mlp_w1[layer].astype(BF16), mlp_w2[layer].astype(BF16)
        for name in o:
            xs[name] = _out_mlp(xs[name], o[name], mod, layer, streams[name][0],
                                norm_mlp[layer][None, :], wo, w1, w2)
    return xs["lat"]
```

```python
import functools
import math

import jax
import jax.numpy as jnp
from jax import lax
from jax.experimental import pallas as pl
from jax.experimental.pallas import tpu as pltpu

F32 = jnp.float32
BF16 = jnp.bfloat16

D_MODEL = 1024
DEPTH = 4
GRID_W = 64
D_FF = 4 * D_MODEL
N_MOD = 6
ROPE_THETA = 10000.0
RMS_EPS = 1e-6
LOG2E = math.log2(math.e)

HEAD_DIM = 64
DIFF_HEADS = 4
DIFF_V_DIM = 128
GQA_Q_HEADS = 8
GQA_KV_HEADS = 2
GQA_GROUP = GQA_Q_HEADS // GQA_KV_HEADS
AB_IN_W = 2304
QK_SCALE_64 = HEAD_DIM ** -0.5

MLA_NOPE = 64
MLA_ROPE = 32
MLA_QK_DIM = MLA_NOPE + MLA_ROPE
MLA_V = 64
MLA_HEADS = 16
MLA_Q_RANK = 512
MLA_KV_RANK = 256
MLA_IN_W = MLA_Q_RANK + MLA_KV_RANK + MLA_ROPE
MLA_SCALE = MLA_QK_DIM ** -0.5

BF16_ROWS = 16
ONES_ROWS = BF16_ROWS
EVEN_V_BLOCK = ONES_ROWS + DIFF_V_DIM
EVEN_V_ROWS = (DIFF_HEADS + GQA_KV_HEADS) * EVEN_V_BLOCK
ODD_V_BLOCK = ONES_ROWS + MLA_V
ODD_V_ROWS = MLA_HEADS * ODD_V_BLOCK
N_UNITS = 16
UNITS_PER_STEP = 4
FEW_KEYS = 512
MAX_ABS_SCORE = 60.0
NORM_MARGIN = 1.05

TOKEN_TILE = 1024
QUERY_TILE = 1024
EXACT_PATH_COLS = 512
MOD_ROWS = 40
FF_CHUNK = 1024
VMEM_LIMIT_BYTES = 56 * 1024 * 1024

_NT = (((1,), (1,)), ((), ()))
_TN = (((0,), (0,)), ((), ()))


def _lambda_init(layer):
    return 0.8 - 0.6 * math.exp(-0.3 * layer)


def _params(n_axes):
    return pltpu.CompilerParams(
        dimension_semantics=("arbitrary",) * n_axes, vmem_limit_bytes=VMEM_LIMIT_BYTES)


def _const_spec(shape):
    return pl.BlockSpec(shape, lambda *_: (0,) * len(shape), pipeline_mode=pl.Buffered(1))


def _rms_modulate(x, gain, shift, scale):
    ms = jnp.mean(x * x, axis=-1, keepdims=True)
    return (x * lax.rsqrt(ms + RMS_EPS) * gain) * (1.0 + scale) + shift


def _rms_rows(y, n):
    return lax.rsqrt(jnp.sum(y * y, axis=0, keepdims=True) * (1.0 / n) + RMS_EPS)


def _rope_rows(x1, x2, cos, sin):
    return x1 * cos - x2 * sin, x2 * cos + x1 * sin


def _ones_block(tokens):
    row = lax.broadcasted_iota(jnp.int32, (ONES_ROWS, tokens), 0)
    return (row == 0).astype(BF16)


def _mod_kernel(a_ref, w_ref, b_ref, o_ref):
    a = a_ref[...]
    a = a / (1.0 + jnp.exp(-a))
    o_ref[0] = jnp.dot(a.astype(BF16), w_ref[0].astype(BF16), preferred_element_type=F32) + b_ref[0]


def _modulation(c, c_ctx, ada_w, ada_b):
    b = c.shape[0]
    rows = jnp.concatenate([c, c_ctx[None, :], jnp.zeros((MOD_ROWS - b - 1, D_MODEL), F32)], axis=0)
    tn = 1024
    out = pl.pallas_call(
        _mod_kernel,
        grid=(DEPTH, N_MOD * D_MODEL // tn),
        in_specs=[
            pl.BlockSpec((MOD_ROWS, D_MODEL), lambda l, j: (0, 0)),
            pl.BlockSpec((1, D_MODEL, tn), lambda l, j: (l, 0, j)),
            pl.BlockSpec((1, 1, tn), lambda l, j: (l, 0, j)),
        ],
        out_specs=pl.BlockSpec((1, MOD_ROWS, tn), lambda l, j: (l, 0, j)),
        out_shape=jax.ShapeDtypeStruct((DEPTH, MOD_ROWS, N_MOD * D_MODEL), F32),
        compiler_params=_params(2),
        name="adaln_modulation",
    )(rows, ada_w, ada_b[:, None, :])
    return out.reshape(DEPTH, MOD_ROWS, N_MOD, D_MODEL)


def _mod_spec(layer, ctx_row):
    if ctx_row is None:
        return pl.BlockSpec((None, None, N_MOD, D_MODEL), lambda b, t: (layer, b, 0, 0))
    return pl.BlockSpec((None, None, N_MOD, D_MODEL), lambda b, t: (layer, ctx_row, 0, 0))


def _inproj_even_kernel(x_ref, mod_ref, g_ref, w_ref, gain_ref, cos_ref, sin_ref, q_ref, k_ref, v_ref):
    h = _rms_modulate(x_ref[0], g_ref[...], mod_ref[0:1, :], mod_ref[1:2, :]).astype(BF16)
    cos = cos_ref[...]
    sin = sin_ref[...]
    tokens = h.shape[0]

    def proj(r0, r1):
        return lax.dot_general(w_ref[r0:r1, :], h, _NT, preferred_element_type=F32)

    def qk_heads(r0, n_heads, dst_ref, d0):
        y = proj(r0, r0 + n_heads * HEAD_DIM)
        for i in range(n_heads):
            yh = y[i * HEAD_DIM:(i + 1) * HEAD_DIM]
            gcol = gain_ref[r0 + i * HEAD_DIM:r0 + (i + 1) * HEAD_DIM, :]
            yn = yh * _rms_rows(yh, HEAD_DIM) * gcol
            o1, o2 = _rope_rows(yn[0:32], yn[32:64], cos, sin)
            base = d0 + i * HEAD_DIM
            dst_ref[0, base:base + 32, :] = o1.astype(BF16)
            dst_ref[0, base + 32:base + 64, :] = o2.astype(BF16)

    qk_heads(0, 8, q_ref, 0)
    qk_heads(512, 8, k_ref, 0)
    qk_heads(1536, 8, q_ref, 512)
    qk_heads(2048, 2, k_ref, 512)
    ones = _ones_block(tokens)
    av = proj(1024, 1536).astype(BF16)
    for i in range(DIFF_HEADS):
        r = i * EVEN_V_BLOCK
        v_ref[0, r:r + ONES_ROWS, :] = ones
        v_ref[0, r + ONES_ROWS:r + EVEN_V_BLOCK, :] = av[i * DIFF_V_DIM:(i + 1) * DIFF_V_DIM]
    bv = proj(2176, 2304).astype(BF16)
    for j in range(GQA_KV_HEADS):
        r = (DIFF_HEADS + j) * EVEN_V_BLOCK
        v_ref[0, r:r + ONES_ROWS, :] = ones
        v_ref[0, r + ONES_ROWS:r + ONES_ROWS + HEAD_DIM, :] = bv[j * HEAD_DIM:(j + 1) * HEAD_DIM]
        v_ref[0, r + ONES_ROWS + HEAD_DIM:r + EVEN_V_BLOCK, :] = jnp.zeros(
            (EVEN_V_BLOCK - ONES_ROWS - HEAD_DIM, tokens), BF16)


def _inproj_even(x, mod, layer, ctx_row, g, w_t, gains, cos_t, sin_t):
    b, n, _ = x.shape
    tm = min(TOKEN_TILE, n)
    tok = lambda rows: pl.BlockSpec((1, rows, tm), lambda i, t: (i, 0, t))
    rope = pl.BlockSpec((32, tm), lambda i, t: (0, t))
    return pl.pallas_call(
        _inproj_even_kernel,
        grid=(b, n // tm),
        in_specs=[
            pl.BlockSpec((1, tm, D_MODEL), lambda i, t: (i, t, 0)),
            _mod_spec(layer, ctx_row),
            _const_spec((1, D_MODEL)),
            _const_spec((AB_IN_W, D_MODEL)),
            _const_spec((AB_IN_W, 1)),
            rope, rope,
        ],
        out_specs=[tok(1024), tok(640), tok(EVEN_V_ROWS)],
        out_shape=[jax.ShapeDtypeStruct((b, 1024, n), BF16),
                   jax.ShapeDtypeStruct((b, 640, n), BF16),
                   jax.ShapeDtypeStruct((b, EVEN_V_ROWS, n), BF16)],
        compiler_params=_params(2),
        name="inproj_even",
    )(x, mod, g, w_t, gains, cos_t, sin_t)


def _inproj_odd_kernel(x_ref, mod_ref, g_ref, w_ref, qn_ref, wq_ref, kvn_ref, wkv_ref, qg_ref, kg_ref,
                       cos_ref, sin_ref, q_ref, k_ref, v_ref):
    h = _rms_modulate(x_ref[0], g_ref[...], mod_ref[0:1, :], mod_ref[1:2, :]).astype(BF16)
    cos = cos_ref[...]
    sin = sin_ref[...]
    y = lax.dot_general(w_ref[...], h, _NT, preferred_element_type=F32)
    qc = y[0:MLA_Q_RANK]
    kvc = y[MLA_Q_RANK:MLA_Q_RANK + MLA_KV_RANK]
    kr = y[MLA_Q_RANK + MLA_KV_RANK:MLA_IN_W]
    qcn = (qc * _rms_rows(qc, MLA_Q_RANK) * qn_ref[...]).astype(BF16)
    kvcn = (kvc * _rms_rows(kvc, MLA_KV_RANK) * kvn_ref[...]).astype(BF16)
    q = jnp.dot(wq_ref[...], qcn, preferred_element_type=F32)
    kv = jnp.dot(wkv_ref[...], kvcn, preferred_element_type=F32)
    kr_ss = jnp.sum(kr * kr, axis=0, keepdims=True)
    qg = qg_ref[...]
    kg = kg_ref[...]
    ones = _ones_block(h.shape[0])
    for i in range(MLA_HEADS):
        b0 = i * MLA_QK_DIM
        qh = q[b0:b0 + MLA_QK_DIM]
        qn = qh * _rms_rows(qh, MLA_QK_DIM) * qg
        o1, o2 = _rope_rows(qn[64:80], qn[80:96], cos, sin)
        q_ref[0, b0:b0 + 64, :] = qn[0:64].astype(BF16)
        q_ref[0, b0 + 64:b0 + 80, :] = o1.astype(BF16)
        q_ref[0, b0 + 80:b0 + 96, :] = o2.astype(BF16)
        kn = kv[i * 128:i * 128 + MLA_NOPE]
        r = lax.rsqrt((jnp.sum(kn * kn, axis=0, keepdims=True) + kr_ss) * (1.0 / MLA_QK_DIM) + RMS_EPS)
        krn = kr * r * kg[64:96]
        o1, o2 = _rope_rows(krn[0:16], krn[16:32], cos, sin)
        k_ref[0, b0:b0 + 64, :] = (kn * r * kg[0:64]).astype(BF16)
        k_ref[0, b0 + 64:b0 + 80, :] = o1.astype(BF16)
        k_ref[0, b0 + 80:b0 + 96, :] = o2.astype(BF16)
        v0 = i * ODD_V_BLOCK
        v_ref[0, v0:v0 + ONES_ROWS, :] = ones
        v_ref[0, v0 + ONES_ROWS:v0 + ODD_V_BLOCK, :] = kv[i * 128 + MLA_NOPE:(i + 1) * 128].astype(BF16)


def _inproj_odd(x, mod, layer, ctx_row, g, w_t, qn, wq_t, kvn, wkv_t, qg, kg, cos_t, sin_t):
    b, n, _ = x.shape
    tm = min(TOKEN_TILE, n)
    tok = lambda rows: pl.BlockSpec((1, rows, tm), lambda i, t: (i, 0, t))
    rope = pl.BlockSpec((16, tm), lambda i, t: (0, t))
    nq = MLA_HEADS * MLA_QK_DIM
    return pl.pallas_call(
        _inproj_odd_kernel,
        grid=(b, n // tm),
        in_specs=[
            pl.BlockSpec((1, tm, D_MODEL), lambda i, t: (i, t, 0)),
            _mod_spec(layer, ctx_row),
            _const_spec((1, D_MODEL)),
            _const_spec((MLA_IN_W, D_MODEL)),
            _const_spec((MLA_Q_RANK, 1)),
            _const_spec((nq, MLA_Q_RANK)),
            _const_spec((MLA_KV_RANK, 1)),
            _const_spec((MLA_HEADS * (MLA_NOPE + MLA_V), MLA_KV_RANK)),
            _const_spec((MLA_QK_DIM, 1)),
            _const_spec((MLA_QK_DIM, 1)),
            rope, rope,
        ],
        out_specs=[tok(nq), tok(nq), tok(ODD_V_ROWS)],
        out_shape=[jax.ShapeDtypeStruct((b, nq, n), BF16),
                   jax.ShapeDtypeStruct((b, nq, n), BF16),
                   jax.ShapeDtypeStruct((b, ODD_V_ROWS, n), BF16)],
        compiler_params=_params(2),
        name="inproj_odd",
    )(x, mod, g, w_t, qn, wq_t, kvn, wkv_t, qg, kg, cos_t, sin_t)


def _attention_kernel(cfg, n_kv, safe_ref, *refs):
    d, v_rows, dv, q_row, k_row, v_row, _, emit, finish = cfg
    q_ref = refs[0]
    k_refs = refs[1:1 + n_kv]
    v_refs = refs[1 + n_kv:1 + 2 * n_kv]
    *extra, o_ref, s0, s1, m0, m1, stage = refs[1 + 2 * n_kv:]
    s_bufs, m_bufs = (s0, s1), (m0, m1)
    bounds = [0]
    for k_ref in k_refs:
        bounds.append(bounds[-1] + k_ref.shape[2])
    tq = q_ref.shape[2]
    cols = s0.shape[1]

    def q_block(i, c0, width):
        return q_ref[0, pl.ds(pl.multiple_of(q_row(i), BF16_ROWS), d), c0:c0 + width]

    def normalised(ox):
        return ox[ONES_ROWS:ONES_ROWS + dv] * (1.0 / ox[0:1])

    def unit_shift_free(i):
        q = q_block(i, 0, tq)
        kr = pl.multiple_of(k_row(i), BF16_ROWS)
        vr = pl.multiple_of(v_row(i), BF16_ROWS)
        ox = None
        for k_ref, v_ref in zip(k_refs, v_refs):
            s = lax.dot_general(k_ref[0, pl.ds(kr, d), :], q, _TN, preferred_element_type=F32)
            oj = jnp.dot(v_ref[0, pl.ds(vr, v_rows), :], jnp.exp2(s).astype(BF16),
                         preferred_element_type=F32)
            ox = oj if ox is None else ox + oj
        emit(i, normalised(ox), 0, tq, stage, o_ref)

    def scores(i, par, c0):
        q = q_block(i, c0, cols)
        kr = pl.multiple_of(k_row(i), BF16_ROWS)
        m = None
        for j, k_ref in enumerate(k_refs):
            s = lax.dot_general(k_ref[0, pl.ds(kr, d), :], q, _TN, preferred_element_type=F32)
            s_bufs[par][bounds[j]:bounds[j + 1], :] = s
            mj = jnp.max(s, axis=0, keepdims=True)
            m = mj if m is None else jnp.maximum(m, mj)
        m_bufs[par][...] = jnp.broadcast_to(m, m_bufs[par].shape)

    def values(i, par, c0):
        m = m_bufs[par][0:1, :]
        vr = pl.multiple_of(v_row(i), BF16_ROWS)
        ox = None
        for j, v_ref in enumerate(v_refs):
            p = jnp.exp2(s_bufs[par][bounds[j]:bounds[j + 1], :] - m).astype(BF16)
            oj = jnp.dot(v_ref[0, pl.ds(vr, v_rows), :], p, preferred_element_type=F32)
            ox = oj if ox is None else ox + oj
        emit(i, normalised(ox), c0, cols, stage, o_ref)

    @pl.when(safe_ref[0] != 0)
    def _():
        per_step = N_UNITS if bounds[-1] <= FEW_KEYS else UNITS_PER_STEP

        def step(t, carry):
            for u in range(per_step):
                unit_shift_free(t * per_step + u)
            return carry

        if per_step == N_UNITS:
            step(0, 0)
        else:
            lax.fori_loop(0, N_UNITS // per_step, step, 0)

    @pl.when(safe_ref[0] == 0)
    def _():
        for c0 in range(0, tq, cols):
            scores(0, 0, c0)

            def step(t, carry):
                @pl.when(t % 2 == 0)
                def _():
                    scores(t + 1, 1, c0)
                    values(t, 0, c0)

                @pl.when(t % 2 == 1)
                def _():
                    scores(t + 1, 0, c0)
                    values(t, 1, c0)
                return carry

            lax.fori_loop(0, N_UNITS - 1, step, 0)
            values(N_UNITS - 1, (N_UNITS - 1) % 2, c0)

    finish(stage, extra, o_ref)


def _attention(cfg, safe, q_t, k_ts, v_ts, extras, extra_specs, tq, name):
    b, rq, nq = q_t.shape
    stage_shape = cfg[6](tq)
    cols = min(tq, EXACT_PATH_COLS)
    nk = sum(k.shape[2] for k in k_ts)
    tile = lambda rows: pl.BlockSpec((1, rows, tq), lambda i, t, safe_ref: (i, 0, t))
    full = lambda a: pl.BlockSpec((1,) + a.shape[1:], lambda i, t, safe_ref: (i, 0, 0))
    return pl.pallas_call(
        functools.partial(_attention_kernel, cfg, len(k_ts)),
        grid_spec=pltpu.PrefetchScalarGridSpec(
            num_scalar_prefetch=1,
            grid=(b, nq // tq),
            in_specs=[tile(rq)] + [full(a) for a in k_ts] + [full(a) for a in v_ts] + extra_specs,
            out_specs=tile(D_MODEL),
            scratch_shapes=[pltpu.VMEM((nk, cols), F32), pltpu.VMEM((nk, cols), F32),
                            pltpu.VMEM((8, cols), F32), pltpu.VMEM((8, cols), F32),
                            pltpu.VMEM(stage_shape, F32)]),
        out_shape=jax.ShapeDtypeStruct((b, D_MODEL, nq), BF16),
        compiler_params=_params(2),
        name=name,
    )(safe, q_t, *k_ts, *v_ts, *extras)


def _even_attention_cfg(lam_init):
    def k_row(i):
        return jnp.where(i < 8, i * HEAD_DIM, 512 + ((i - 8) // GQA_GROUP) * HEAD_DIM)

    def v_row(i):
        return jnp.where(i < 8, i // 2, DIFF_HEADS + (i - 8) // GQA_GROUP) * EVEN_V_BLOCK

    def emit(i, o, c0, width, stage, o_ref):
        stage[i, :, c0:c0 + width] = o

    def finish(stage, extra, o_ref):
        dl_ref, sub_ref = extra
        dl = dl_ref[...]
        lam = (jnp.exp(jnp.sum(dl[0:1] * dl[1:2], axis=1, keepdims=True))
               - jnp.exp(jnp.sum(dl[2:3] * dl[3:4], axis=1, keepdims=True)) + lam_init)
        for h in range(DIFF_HEADS):
            o = stage[2 * h] - lam * stage[2 * h + 1]
            o = o * _rms_rows(o, DIFF_V_DIM) * sub_ref[...]
            o_ref[0, h * DIFF_V_DIM:(h + 1) * DIFF_V_DIM, :] = o.astype(BF16)
        for g in range(GQA_Q_HEADS):
            r = 512 + g * HEAD_DIM
            o_ref[0, r:r + HEAD_DIM, :] = stage[8 + g, 0:HEAD_DIM, :].astype(BF16)

    return (HEAD_DIM, EVEN_V_BLOCK, DIFF_V_DIM, lambda i: i * HEAD_DIM, k_row, v_row,
            lambda tq: (N_UNITS, DIFF_V_DIM, tq), emit, finish)


def _odd_attention_cfg():
    def emit(i, o, c0, width, stage, o_ref):
        o_ref[0, pl.ds(pl.multiple_of(i * MLA_V, BF16_ROWS), MLA_V), c0:c0 + width] = o.astype(BF16)

    def finish(stage, extra, o_ref):
        pass

    return (MLA_QK_DIM, ODD_V_BLOCK, MLA_V, lambda i: i * MLA_QK_DIM, lambda i: i * MLA_QK_DIM,
            lambda i: i * ODD_V_BLOCK, lambda tq: (8, 128), emit, finish)


def _out_mlp_kernel(x_ref, o_ref, mod_ref, g_ref, wo_ref, w1_ref, w2_ref, out_ref):
    y = lax.dot_general(o_ref[0], wo_ref[...], _TN, preferred_element_type=F32)
    x1 = x_ref[0] + mod_ref[2:3, :] * y
    h = _rms_modulate(x1, g_ref[...], mod_ref[3:4, :], mod_ref[4:5, :]).astype(BF16)
    acc = jnp.zeros(x1.shape, F32)
    for c in range(D_FF // FF_CHUNK):
        u = jnp.dot(h, w1_ref[:, c * FF_CHUNK:(c + 1) * FF_CHUNK], preferred_element_type=F32)
        a = jnp.square(jnp.maximum(u, 0.0)).astype(BF16)
        acc = acc + jnp.dot(a, w2_ref[c * FF_CHUNK:(c + 1) * FF_CHUNK, :], preferred_element_type=F32)
    out_ref[0] = x1 + mod_ref[5:6, :] * acc


def _out_mlp(x, o_t, mod, layer, ctx_row, g, wo, w1, w2):
    b, n, _ = x.shape
    tm = min(TOKEN_TILE, n)
    return pl.pallas_call(
        _out_mlp_kernel,
        grid=(b, n // tm),
        in_specs=[
            pl.BlockSpec((1, tm, D_MODEL), lambda i, t: (i, t, 0)),
            pl.BlockSpec((1, D_MODEL, tm), lambda i, t: (i, 0, t)),
            _mod_spec(layer, ctx_row),
            _const_spec((1, D_MODEL)),
            _const_spec((D_MODEL, D_MODEL)),
            _const_spec((D_MODEL, D_FF)),
            _const_spec((D_FF, D_MODEL)),
        ],
        out_specs=pl.BlockSpec((1, tm, D_MODEL), lambda i, t: (i, t, 0)),
        out_shape=jax.ShapeDtypeStruct((b, n, D_MODEL), F32),
        compiler_params=_params(2),
        name="outproj_mlp",
    )(x, o_t, mod, g, wo, w1, w2)


def _rope_tables(n_lat, rot_dim):
    rows = n_lat // GRID_W
    n_freq = rot_dim // 4
    inv_freq = ROPE_THETA ** (-jnp.arange(n_freq, dtype=F32) / n_freq)
    row = jnp.repeat(jnp.arange(rows, dtype=F32), GRID_W)
    col = jnp.tile(jnp.arange(GRID_W, dtype=F32), rows)
    ang = jnp.concatenate([row[:, None] * inv_freq, col[:, None] * inv_freq], axis=-1)
    return jnp.cos(ang).T, jnp.sin(ang).T


def _identity_rope(n_tokens, rot_dim):
    return jnp.ones((rot_dim // 2, n_tokens), F32), jnp.zeros((rot_dim // 2, n_tokens), F32)


def _col(v):
    return v.astype(F32)[:, None]


def _max_sq(gain):
    return jnp.max(jnp.square(gain.astype(F32)))


def kernel(x, c, ctx, c_ctx, ada_w, ada_b, norm_mix, norm_mlp, mlp_w1, mlp_w2, ab_w_in, ab_w_out,
           diff_qk_norm, diff_lambda, diff_subln, gqa_qk_norm, mla_w_in, mla_q_norm, mla_w_q_up,
           mla_kv_norm, mla_w_kv_up, mla_qk_norm, mla_w_out):
    b, n_lat, d = x.shape
    n_ctx = ctx.shape[1]
    assert d == D_MODEL and b < MOD_ROWS
    assert n_lat % QUERY_TILE == 0 and n_lat % TOKEN_TILE == 0 and n_ctx <= TOKEN_TILE

    mod = _modulation(c, c_ctx, ada_w, ada_b)
    rope_lat = {HEAD_DIM: _rope_tables(n_lat, HEAD_DIM), MLA_ROPE: _rope_tables(n_lat, MLA_ROPE)}
    rope_ctx = {HEAD_DIM: _identity_rope(n_ctx, HEAD_DIM), MLA_ROPE: _identity_rope(n_ctx, MLA_ROPE)}
    streams = {"lat": (None, rope_lat), "ctx": (b, rope_ctx)}
    xs = {"lat": x, "ctx": ctx}

    for layer in range(DEPTH):
        update_ctx = layer < DEPTH - 1
        i = layer // 2
        g_mix = norm_mix[layer][None, :]
        qkv = {}
        if layer % 2 == 0:
            lam_init = _lambda_init(layer)
            ones = jnp.ones((HEAD_DIM,), F32)
            q_scale = QK_SCALE_64 * LOG2E
            gains = _col(jnp.concatenate([
                jnp.tile(diff_qk_norm[i, 0] * q_scale, 8), jnp.tile(diff_qk_norm[i, 1], 8),
                jnp.tile(ones, 8),
                jnp.tile(gqa_qk_norm[i, 0] * q_scale, 8), jnp.tile(gqa_qk_norm[i, 1], 2),
                jnp.tile(ones, 2)]))
            bound2 = HEAD_DIM ** 2 * jnp.maximum(
                _max_sq(diff_qk_norm[i, 0] * q_scale) * _max_sq(diff_qk_norm[i, 1]),
                _max_sq(gqa_qk_norm[i, 0] * q_scale) * _max_sq(gqa_qk_norm[i, 1]))
            w_t = ab_w_in[i].T.astype(BF16)
            for name, (ctx_row, rope) in streams.items():
                qkv[name] = _inproj_even(xs[name], mod, layer, ctx_row, g_mix, w_t, gains, *rope[HEAD_DIM])
            cfg = _even_attention_cfg(lam_init)
            extras = [diff_lambda[i], _col(diff_subln[i] * (1.0 - lam_init))]
            extra_specs = [_const_spec((4, HEAD_DIM)), _const_spec((DIFF_V_DIM, 1))]
            w_out = ab_w_out[i]
        else:
            q_gain = mla_qk_norm[i, 0] * (MLA_SCALE * LOG2E)
            bound2 = MLA_QK_DIM ** 2 * _max_sq(q_gain) * _max_sq(mla_qk_norm[i, 1])
            weights = (mla_w_in[i].T.astype(BF16), _col(mla_q_norm[i]), mla_w_q_up[i].T.astype(BF16),
                       _col(mla_kv_norm[i]), mla_w_kv_up[i].T.astype(BF16),
                       _col(q_gain), _col(mla_qk_norm[i, 1]))
            for name, (ctx_row, rope) in streams.items():
                qkv[name] = _inproj_odd(xs[name], mod, layer, ctx_row, g_mix, *weights, *rope[MLA_ROPE])
            cfg = _odd_attention_cfg()
            extras, extra_specs = [], []
            w_out = mla_w_out[i]

        safe = (bound2 * NORM_MARGIN <= MAX_ABS_SCORE ** 2).astype(jnp.int32).reshape(1)
        o = {"lat": _attention(cfg, safe, qkv["lat"][0], [qkv["lat"][1], qkv["ctx"][1]],
                               [qkv["lat"][2], qkv["ctx"][2]], extras, extra_specs, QUERY_TILE,
                               "attention_lat")}
        if update_ctx:
            o["ctx"] = _attention(cfg, safe, qkv["ctx"][0], [qkv["ctx"][1]], [qkv["ctx"][2]], extras,
                                  extra_specs, n_ctx, "attention_ctx")
        wo, w1, w2 = w_out.astype(BF16), mlp_w1[layer].astype(BF16), mlp_w2[layer].astype(BF16)
        for name in o:
            xs[name] = _out_mlp(xs[name], o[name], mod, layer, streams[name][0],
                                norm_mlp[layer][None, :], wo, w1, w2)
    return xs["lat"]
```

```python
import functools
import math

import jax
import jax.numpy as jnp
from jax import lax
from jax.experimental import pallas as pl
from jax.experimental.pallas import tpu as pltpu

F32 = jnp.float32
BF16 = jnp.bfloat16

D_MODEL = 1024
DEPTH = 4
GRID_W = 64
D_FF = 4 * D_MODEL
N_MOD = 6
ROPE_THETA = 10000.0
RMS_EPS = 1e-6
LOG2E = math.log2(math.e)

HEAD_DIM = 64
DIFF_HEADS = 4
DIFF_V_DIM = 128
GQA_Q_HEADS = 8
GQA_KV_HEADS = 2
GQA_GROUP = GQA_Q_HEADS // GQA_KV_HEADS
AB_IN_W = 2304
QK_SCALE_64 = HEAD_DIM ** -0.5

MLA_NOPE = 64
MLA_ROPE = 32
MLA_QK_DIM = MLA_NOPE + MLA_ROPE
MLA_V = 64
MLA_HEADS = 16
MLA_Q_RANK = 512
MLA_KV_RANK = 256
MLA_IN_W = MLA_Q_RANK + MLA_KV_RANK + MLA_ROPE
MLA_SCALE = MLA_QK_DIM ** -0.5

BF16_ROWS = 16
ONES_ROWS = BF16_ROWS
EVEN_V_BLOCK = ONES_ROWS + DIFF_V_DIM
EVEN_V_ROWS = (DIFF_HEADS + GQA_KV_HEADS) * EVEN_V_BLOCK
ODD_V_BLOCK = ONES_ROWS + MLA_V
ODD_V_ROWS = MLA_HEADS * ODD_V_BLOCK
N_UNITS = 16
UNITS_PER_STEP = 4
FEW_KEYS = 512
KEY_CHUNK = 512
LOOKAHEAD = 2
FEW_KEYS_LOOKAHEAD = 8
MAX_ABS_SCORE = 60.0
NORM_MARGIN = 1.05

TOKEN_TILE = 1024
QUERY_TILE = 1024
EXACT_PATH_COLS = 512
MOD_ROWS = 40
FF_CHUNK = 1024
VMEM_LIMIT_BYTES = 56 * 1024 * 1024

_NT = (((1,), (1,)), ((), ()))
_TN = (((0,), (0,)), ((), ()))


def _lambda_init(layer):
    return 0.8 - 0.6 * math.exp(-0.3 * layer)


def _params(n_axes):
    return pltpu.CompilerParams(
        dimension_semantics=("arbitrary",) * n_axes, vmem_limit_bytes=VMEM_LIMIT_BYTES)


def _const_spec(shape):
    return pl.BlockSpec(shape, lambda *_: (0,) * len(shape), pipeline_mode=pl.Buffered(1))


def _rms_modulate(x, gain, shift, scale):
    ms = jnp.mean(x * x, axis=-1, keepdims=True)
    return (x * lax.rsqrt(ms + RMS_EPS) * gain) * (1.0 + scale) + shift


def _rms_rows(y, n):
    return lax.rsqrt(jnp.sum(y * y, axis=0, keepdims=True) * (1.0 / n) + RMS_EPS)


def _rope_rows(x1, x2, cos, sin):
    return x1 * cos - x2 * sin, x2 * cos + x1 * sin


def _ones_block(tokens):
    row = lax.broadcasted_iota(jnp.int32, (ONES_ROWS, tokens), 0)
    return (row == 0).astype(BF16)


def _mod_kernel(a_ref, w_ref, b_ref, o_ref):
    a = a_ref[...]
    a = a / (1.0 + jnp.exp(-a))
    o_ref[0] = jnp.dot(a.astype(BF16), w_ref[0].astype(BF16), preferred_element_type=F32) + b_ref[0]


def _modulation(c, c_ctx, ada_w, ada_b):
    b = c.shape[0]
    rows = jnp.concatenate([c, c_ctx[None, :], jnp.zeros((MOD_ROWS - b - 1, D_MODEL), F32)], axis=0)
    tn = 1024
    out = pl.pallas_call(
        _mod_kernel,
        grid=(DEPTH, N_MOD * D_MODEL // tn),
        in_specs=[
            pl.BlockSpec((MOD_ROWS, D_MODEL), lambda l, j: (0, 0)),
            pl.BlockSpec((1, D_MODEL, tn), lambda l, j: (l, 0, j)),
            pl.BlockSpec((1, 1, tn), lambda l, j: (l, 0, j)),
        ],
        out_specs=pl.BlockSpec((1, MOD_ROWS, tn), lambda l, j: (l, 0, j)),
        out_shape=jax.ShapeDtypeStruct((DEPTH, MOD_ROWS, N_MOD * D_MODEL), F32),
        compiler_params=_params(2),
        name="adaln_modulation",
    )(rows, ada_w, ada_b[:, None, :])
    return out.reshape(DEPTH, MOD_ROWS, N_MOD, D_MODEL)


def _mod_spec(layer, ctx_row):
    if ctx_row is None:
        return pl.BlockSpec((None, None, N_MOD, D_MODEL), lambda b, t: (layer, b, 0, 0))
    return pl.BlockSpec((None, None, N_MOD, D_MODEL), lambda b, t: (layer, ctx_row, 0, 0))


def _inproj_even_kernel(x_ref, mod_ref, g_ref, w_ref, gain_ref, cos_ref, sin_ref, q_ref, k_ref, v_ref):
    h = _rms_modulate(x_ref[0], g_ref[...], mod_ref[0:1, :], mod_ref[1:2, :]).astype(BF16)
    cos = cos_ref[...]
    sin = sin_ref[...]
    tokens = h.shape[0]

    def proj(r0, r1):
        return lax.dot_general(w_ref[r0:r1, :], h, _NT, preferred_element_type=F32)

    def qk_heads(r0, n_heads, dst_ref, d0):
        y = proj(r0, r0 + n_heads * HEAD_DIM)
        for i in range(n_heads):
            yh = y[i * HEAD_DIM:(i + 1) * HEAD_DIM]
            gcol = gain_ref[r0 + i * HEAD_DIM:r0 + (i + 1) * HEAD_DIM, :]
            yn = yh * _rms_rows(yh, HEAD_DIM) * gcol
            o1, o2 = _rope_rows(yn[0:32], yn[32:64], cos, sin)
            base = d0 + i * HEAD_DIM
            dst_ref[0, base:base + 32, :] = o1.astype(BF16)
            dst_ref[0, base + 32:base + 64, :] = o2.astype(BF16)

    qk_heads(0, 8, q_ref, 0)
    qk_heads(512, 8, k_ref, 0)
    qk_heads(1536, 8, q_ref, 512)
    qk_heads(2048, 2, k_ref, 512)
    ones = _ones_block(tokens)
    av = proj(1024, 1536).astype(BF16)
    for i in range(DIFF_HEADS):
        r = i * EVEN_V_BLOCK
        v_ref[0, r:r + ONES_ROWS, :] = ones
        v_ref[0, r + ONES_ROWS:r + EVEN_V_BLOCK, :] = av[i * DIFF_V_DIM:(i + 1) * DIFF_V_DIM]
    bv = proj(2176, 2304).astype(BF16)
    for j in range(GQA_KV_HEADS):
        r = (DIFF_HEADS + j) * EVEN_V_BLOCK
        v_ref[0, r:r + ONES_ROWS, :] = ones
        v_ref[0, r + ONES_ROWS:r + ONES_ROWS + HEAD_DIM, :] = bv[j * HEAD_DIM:(j + 1) * HEAD_DIM]
        v_ref[0, r + ONES_ROWS + HEAD_DIM:r + EVEN_V_BLOCK, :] = jnp.zeros(
            (EVEN_V_BLOCK - ONES_ROWS - HEAD_DIM, tokens), BF16)


def _inproj_even(x, mod, layer, ctx_row, g, w_t, gains, cos_t, sin_t):
    b, n, _ = x.shape
    tm = min(TOKEN_TILE, n)
    tok = lambda rows: pl.BlockSpec((1, rows, tm), lambda i, t: (i, 0, t))
    rope = pl.BlockSpec((32, tm), lambda i, t: (0, t))
    return pl.pallas_call(
        _inproj_even_kernel,
        grid=(b, n // tm),
        in_specs=[
            pl.BlockSpec((1, tm, D_MODEL), lambda i, t: (i, t, 0)),
            _mod_spec(layer, ctx_row),
            _const_spec((1, D_MODEL)),
            _const_spec((AB_IN_W, D_MODEL)),
            _const_spec((AB_IN_W, 1)),
            rope, rope,
        ],
        out_specs=[tok(1024), tok(640), tok(EVEN_V_ROWS)],
        out_shape=[jax.ShapeDtypeStruct((b, 1024, n), BF16),
                   jax.ShapeDtypeStruct((b, 640, n), BF16),
                   jax.ShapeDtypeStruct((b, EVEN_V_ROWS, n), BF16)],
        compiler_params=_params(2),
        name="inproj_even",
    )(x, mod, g, w_t, gains, cos_t, sin_t)


def _inproj_odd_kernel(x_ref, mod_ref, g_ref, w_ref, qn_ref, wq_ref, kvn_ref, wkv_ref, qg_ref, kg_ref,
                       cos_ref, sin_ref, q_ref, k_ref, v_ref):
    h = _rms_modulate(x_ref[0], g_ref[...], mod_ref[0:1, :], mod_ref[1:2, :]).astype(BF16)
    cos = cos_ref[...]
    sin = sin_ref[...]
    y = lax.dot_general(w_ref[...], h, _NT, preferred_element_type=F32)
    qc = y[0:MLA_Q_RANK]
    kvc = y[MLA_Q_RANK:MLA_Q_RANK + MLA_KV_RANK]
    kr = y[MLA_Q_RANK + MLA_KV_RANK:MLA_IN_W]
    qcn = (qc * _rms_rows(qc, MLA_Q_RANK) * qn_ref[...]).astype(BF16)
    kvcn = (kvc * _rms_rows(kvc, MLA_KV_RANK) * kvn_ref[...]).astype(BF16)
    q = jnp.dot(wq_ref[...], qcn, preferred_element_type=F32)
    kv = jnp.dot(wkv_ref[...], kvcn, preferred_element_type=F32)
    kr_ss = jnp.sum(kr * kr, axis=0, keepdims=True)
    qg = qg_ref[...]
    kg = kg_ref[...]
    ones = _ones_block(h.shape[0])
    for i in range(MLA_HEADS):
        b0 = i * MLA_QK_DIM
        qh = q[b0:b0 + MLA_QK_DIM]
        qn = qh * _rms_rows(qh, MLA_QK_DIM) * qg
        o1, o2 = _rope_rows(qn[64:80], qn[80:96], cos, sin)
        q_ref[0, b0:b0 + 64, :] = qn[0:64].astype(BF16)
        q_ref[0, b0 + 64:b0 + 80, :] = o1.astype(BF16)
        q_ref[0, b0 + 80:b0 + 96, :] = o2.astype(BF16)
        kn = kv[i * 128:i * 128 + MLA_NOPE]
        r = lax.rsqrt((jnp.sum(kn * kn, axis=0, keepdims=True) + kr_ss) * (1.0 / MLA_QK_DIM) + RMS_EPS)
        krn = kr * r * kg[64:96]
        o1, o2 = _rope_rows(krn[0:16], krn[16:32], cos, sin)
        k_ref[0, b0:b0 + 64, :] = (kn * r * kg[0:64]).astype(BF16)
        k_ref[0, b0 + 64:b0 + 80, :] = o1.astype(BF16)
        k_ref[0, b0 + 80:b0 + 96, :] = o2.astype(BF16)
        v0 = i * ODD_V_BLOCK
        v_ref[0, v0:v0 + ONES_ROWS, :] = ones
        v_ref[0, v0 + ONES_ROWS:v0 + ODD_V_BLOCK, :] = kv[i * 128 + MLA_NOPE:(i + 1) * 128].astype(BF16)


def _inproj_odd(x, mod, layer, ctx_row, g, w_t, qn, wq_t, kvn, wkv_t, qg, kg, cos_t, sin_t):
    b, n, _ = x.shape
    tm = min(TOKEN_TILE, n)
    tok = lambda rows: pl.BlockSpec((1, rows, tm), lambda i, t: (i, 0, t))
    rope = pl.BlockSpec((16, tm), lambda i, t: (0, t))
    nq = MLA_HEADS * MLA_QK_DIM
    return pl.pallas_call(
        _inproj_odd_kernel,
        grid=(b, n // tm),
        in_specs=[
            pl.BlockSpec((1, tm, D_MODEL), lambda i, t: (i, t, 0)),
            _mod_spec(layer, ctx_row),
            _const_spec((1, D_MODEL)),
            _const_spec((MLA_IN_W, D_MODEL)),
            _const_spec((MLA_Q_RANK, 1)),
            _const_spec((nq, MLA_Q_RANK)),
            _const_spec((MLA_KV_RANK, 1)),
            _const_spec((MLA_HEADS * (MLA_NOPE + MLA_V), MLA_KV_RANK)),
            _const_spec((MLA_QK_DIM, 1)),
            _const_spec((MLA_QK_DIM, 1)),
            rope, rope,
        ],
        out_specs=[tok(nq), tok(nq), tok(ODD_V_ROWS)],
        out_shape=[jax.ShapeDtypeStruct((b, nq, n), BF16),
                   jax.ShapeDtypeStruct((b, nq, n), BF16),
                   jax.ShapeDtypeStruct((b, ODD_V_ROWS, n), BF16)],
        compiler_params=_params(2),
        name="inproj_odd",
    )(x, mod, g, w_t, qn, wq_t, kvn, wkv_t, qg, kg, cos_t, sin_t)


def _attention_kernel(cfg, n_kv, safe_ref, *refs):
    d, v_rows, dv, q_row, k_row, v_row, _, emit, finish = cfg
    q_ref = refs[0]
    k_refs = refs[1:1 + n_kv]
    v_refs = refs[1 + n_kv:1 + 2 * n_kv]
    *extra, o_ref, s0, s1, m0, m1, stage = refs[1 + 2 * n_kv:]
    s_bufs, m_bufs = (s0, s1), (m0, m1)
    bounds = [0]
    for k_ref in k_refs:
        bounds.append(bounds[-1] + k_ref.shape[2])
    tq = q_ref.shape[2]
    cols = s0.shape[1]

    def q_block(i, c0, width):
        return q_ref[0, pl.ds(pl.multiple_of(q_row(i), BF16_ROWS), d), c0:c0 + width]

    def normalised(ox):
        return ox[ONES_ROWS:ONES_ROWS + dv] * (1.0 / ox[0:1])

    def shift_free_units(first, count, lookahead):
        tasks = []
        for i in (first + u for u in range(count)):
            q = q_block(i, 0, tq)
            kr = pl.multiple_of(k_row(i), BF16_ROWS)
            vr = pl.multiple_of(v_row(i), BF16_ROWS)
            for k_ref, v_ref in zip(k_refs, v_refs):
                for k0 in range(0, k_ref.shape[2], KEY_CHUNK):
                    ks = slice(k0, min(k0 + KEY_CHUNK, k_ref.shape[2]))
                    tasks.append((i, q, k_ref.at[0, pl.ds(kr, d), ks], v_ref.at[0, pl.ds(vr, v_rows), ks]))
        per_unit = len(tasks) // count
        scores, acc = {}, None
        for n in range(len(tasks) + lookahead):
            if n < len(tasks):
                _, q, k_blk, _ = tasks[n]
                scores[n] = lax.dot_general(k_blk[...], q, _TN, preferred_element_type=F32)
            n_val = n - lookahead
            if n_val >= 0:
                i, _, _, v_blk = tasks[n_val]
                oj = jnp.dot(v_blk[...], jnp.exp2(scores.pop(n_val)).astype(BF16),
                             preferred_element_type=F32)
                acc = oj if n_val % per_unit == 0 else acc + oj
                if n_val % per_unit == per_unit - 1:
                    emit(i, normalised(acc), 0, tq, stage, o_ref)

    def scores(i, par, c0):
        q = q_block(i, c0, cols)
        kr = pl.multiple_of(k_row(i), BF16_ROWS)
        m = None
        for j, k_ref in enumerate(k_refs):
            s = lax.dot_general(k_ref[0, pl.ds(kr, d), :], q, _TN, preferred_element_type=F32)
            s_bufs[par][bounds[j]:bounds[j + 1], :] = s
            mj = jnp.max(s, axis=0, keepdims=True)
            m = mj if m is None else jnp.maximum(m, mj)
        m_bufs[par][...] = jnp.broadcast_to(m, m_bufs[par].shape)

    def values(i, par, c0):
        m = m_bufs[par][0:1, :]
        vr = pl.multiple_of(v_row(i), BF16_ROWS)
        ox = None
        for j, v_ref in enumerate(v_refs):
            p = jnp.exp2(s_bufs[par][bounds[j]:bounds[j + 1], :] - m).astype(BF16)
            oj = jnp.dot(v_ref[0, pl.ds(vr, v_rows), :], p, preferred_element_type=F32)
            ox = oj if ox is None else ox + oj
        emit(i, normalised(ox), c0, cols, stage, o_ref)

    @pl.when(safe_ref[0] != 0)
    def _():
        if bounds[-1] <= FEW_KEYS:
            shift_free_units(0, N_UNITS, FEW_KEYS_LOOKAHEAD)
        else:
            def step(t, carry):
                shift_free_units(t * UNITS_PER_STEP, UNITS_PER_STEP, LOOKAHEAD)
                return carry

            lax.fori_loop(0, N_UNITS // UNITS_PER_STEP, step, 0)

    @pl.when(safe_ref[0] == 0)
    def _():
        for c0 in range(0, tq, cols):
            scores(0, 0, c0)

            def step(t, carry):
                @pl.when(t % 2 == 0)
                def _():
                    scores(t + 1, 1, c0)
                    values(t, 0, c0)

                @pl.when(t % 2 == 1)
                def _():
                    scores(t + 1, 0, c0)
                    values(t, 1, c0)
                return carry

            lax.fori_loop(0, N_UNITS - 1, step, 0)
            values(N_UNITS - 1, (N_UNITS - 1) % 2, c0)

    finish(stage, extra, o_ref)


def _attention(cfg, safe, q_t, k_ts, v_ts, extras, extra_specs, tq, name):
    b, rq, nq = q_t.shape
    stage_shape = cfg[6](tq)
    cols = min(tq, EXACT_PATH_COLS)
    nk = sum(k.shape[2] for k in k_ts)
    tile = lambda rows: pl.BlockSpec((1, rows, tq), lambda i, t, safe_ref: (i, 0, t))
    full = lambda a: pl.BlockSpec((1,) + a.shape[1:], lambda i, t, safe_ref: (i, 0, 0))
    return pl.pallas_call(
        functools.partial(_attention_kernel, cfg, len(k_ts)),
        grid_spec=pltpu.PrefetchScalarGridSpec(
            num_scalar_prefetch=1,
            grid=(b, nq // tq),
            in_specs=[tile(rq)] + [full(a) for a in k_ts] + [full(a) for a in v_ts] + extra_specs,
            out_specs=tile(D_MODEL),
            scratch_shapes=[pltpu.VMEM((nk, cols), F32), pltpu.VMEM((nk, cols), F32),
                            pltpu.VMEM((8, cols), F32), pltpu.VMEM((8, cols), F32),
                            pltpu.VMEM(stage_shape, F32)]),
        out_shape=jax.ShapeDtypeStruct((b, D_MODEL, nq), BF16),
        compiler_params=_params(2),
        name=name,
    )(safe, q_t, *k_ts, *v_ts, *extras)


def _even_attention_cfg(lam_init):
    def k_row(i):
        return jnp.where(i < 8, i * HEAD_DIM, 512 + ((i - 8) // GQA_GROUP) * HEAD_DIM)

    def v_row(i):
        return jnp.where(i < 8, i // 2, DIFF_HEADS + (i - 8) // GQA_GROUP) * EVEN_V_BLOCK

    def emit(i, o, c0, width, stage, o_ref):
        stage[i, :, c0:c0 + width] = o

    def finish(stage, extra, o_ref):
        dl_ref, sub_ref = extra
        dl = dl_ref[...]
        lam = (jnp.exp(jnp.sum(dl[0:1] * dl[1:2], axis=1, keepdims=True))
               - jnp.exp(jnp.sum(dl[2:3] * dl[3:4], axis=1, keepdims=True)) + lam_init)
        for h in range(DIFF_HEADS):
            o = stage[2 * h] - lam * stage[2 * h + 1]
            o = o * _rms_rows(o, DIFF_V_DIM) * sub_ref[...]
            o_ref[0, h * DIFF_V_DIM:(h + 1) * DIFF_V_DIM, :] = o.astype(BF16)
        for g in range(GQA_Q_HEADS):
            r = 512 + g * HEAD_DIM
            o_ref[0, r:r + HEAD_DIM, :] = stage[8 + g, 0:HEAD_DIM, :].astype(BF16)

    return (HEAD_DIM, EVEN_V_BLOCK, DIFF_V_DIM, lambda i: i * HEAD_DIM, k_row, v_row,
            lambda tq: (N_UNITS, DIFF_V_DIM, tq), emit, finish)


def _odd_attention_cfg():
    def emit(i, o, c0, width, stage, o_ref):
        o_ref[0, pl.ds(pl.multiple_of(i * MLA_V, BF16_ROWS), MLA_V), c0:c0 + width] = o.astype(BF16)

    def finish(stage, extra, o_ref):
        pass

    return (MLA_QK_DIM, ODD_V_BLOCK, MLA_V, lambda i: i * MLA_QK_DIM, lambda i: i * MLA_QK_DIM,
            lambda i: i * ODD_V_BLOCK, lambda tq: (8, 128), emit, finish)


def _out_mlp_kernel(x_ref, o_ref, mod_ref, g_ref, wo_ref, w1_ref, w2_ref, out_ref):
    y = lax.dot_general(o_ref[0], wo_ref[...], _TN, preferred_element_type=F32)
    x1 = x_ref[0] + mod_ref[2:3, :] * y
    h = _rms_modulate(x1, g_ref[...], mod_ref[3:4, :], mod_ref[4:5, :]).astype(BF16)
    acc = jnp.zeros(x1.shape, F32)
    for c in range(D_FF // FF_CHUNK):
        u = jnp.dot(h, w1_ref[:, c * FF_CHUNK:(c + 1) * FF_CHUNK], preferred_element_type=F32)
        a = jnp.square(jnp.maximum(u, 0.0)).astype(BF16)
        acc = acc + jnp.dot(a, w2_ref[c * FF_CHUNK:(c + 1) * FF_CHUNK, :], preferred_element_type=F32)
    out_ref[0] = x1 + mod_ref[5:6, :] * acc


def _out_mlp(x, o_t, mod, layer, ctx_row, g, wo, w1, w2):
    b, n, _ = x.shape
    tm = min(TOKEN_TILE, n)
    return pl.pallas_call(
        _out_mlp_kernel,
        grid=(b, n // tm),
        in_specs=[
            pl.BlockSpec((1, tm, D_MODEL), lambda i, t: (i, t, 0)),
            pl.BlockSpec((1, D_MODEL, tm), lambda i, t: (i, 0, t)),
            _mod_spec(layer, ctx_row),
            _const_spec((1, D_MODEL)),
            _const_spec((D_MODEL, D_MODEL)),
            _const_spec((D_MODEL, D_FF)),
            _const_spec((D_FF, D_MODEL)),
        ],
        out_specs=pl.BlockSpec((1, tm, D_MODEL), lambda i, t: (i, t, 0)),
        out_shape=jax.ShapeDtypeStruct((b, n, D_MODEL), F32),
        compiler_params=_params(2),
        name="outproj_mlp",
    )(x, o_t, mod, g, wo, w1, w2)


def _rope_tables(n_lat, rot_dim):
    rows = n_lat // GRID_W
    n_freq = rot_dim // 4
    inv_freq = ROPE_THETA ** (-jnp.arange(n_freq, dtype=F32) / n_freq)
    row = jnp.repeat(jnp.arange(rows, dtype=F32), GRID_W)
    col = jnp.tile(jnp.arange(GRID_W, dtype=F32), rows)
    ang = jnp.concatenate([row[:, None] * inv_freq, col[:, None] * inv_freq], axis=-1)
    return jnp.cos(ang).T, jnp.sin(ang).T


def _identity_rope(n_tokens, rot_dim):
    return jnp.ones((rot_dim // 2, n_tokens), F32), jnp.zeros((rot_dim // 2, n_tokens), F32)


def _col(v):
    return v.astype(F32)[:, None]


def _max_sq(gain):
    return jnp.max(jnp.square(gain.astype(F32)))


def kernel(x, c, ctx, c_ctx, ada_w, ada_b, norm_mix, norm_mlp, mlp_w1, mlp_w2, ab_w_in, ab_w_out,
           diff_qk_norm, diff_lambda, diff_subln, gqa_qk_norm, mla_w_in, mla_q_norm, mla_w_q_up,
           mla_kv_norm, mla_w_kv_up, mla_qk_norm, mla_w_out):
    b, n_lat, d = x.shape
    n_ctx = ctx.shape[1]
    assert d == D_MODEL and b < MOD_ROWS
    assert n_lat % QUERY_TILE == 0 and n_lat % TOKEN_TILE == 0 and n_ctx <= TOKEN_TILE

    mod = _modulation(c, c_ctx, ada_w, ada_b)
    rope_lat = {HEAD_DIM: _rope_tables(n_lat, HEAD_DIM), MLA_ROPE: _rope_tables(n_lat, MLA_ROPE)}
    rope_ctx = {HEAD_DIM: _identity_rope(n_ctx, HEAD_DIM), MLA_ROPE: _identity_rope(n_ctx, MLA_ROPE)}
    streams = {"lat": (None, rope_lat), "ctx": (b, rope_ctx)}
    xs = {"lat": x, "ctx": ctx}

    for layer in range(DEPTH):
        update_ctx = layer < DEPTH - 1
        i = layer // 2
        g_mix = norm_mix[layer][None, :]
        qkv = {}
        if layer % 2 == 0:
            lam_init = _lambda_init(layer)
            ones = jnp.ones((HEAD_DIM,), F32)
            q_scale = QK_SCALE_64 * LOG2E
            gains = _col(jnp.concatenate([
                jnp.tile(diff_qk_norm[i, 0] * q_scale, 8), jnp.tile(diff_qk_norm[i, 1], 8),
                jnp.tile(ones, 8),
                jnp.tile(gqa_qk_norm[i, 0] * q_scale, 8), jnp.tile(gqa_qk_norm[i, 1], 2),
                jnp.tile(ones, 2)]))
            bound2 = HEAD_DIM ** 2 * jnp.maximum(
                _max_sq(diff_qk_norm[i, 0] * q_scale) * _max_sq(diff_qk_norm[i, 1]),
                _max_sq(gqa_qk_norm[i, 0] * q_scale) * _max_sq(gqa_qk_norm[i, 1]))
            w_t = ab_w_in[i].T.astype(BF16)
            for name, (ctx_row, rope) in streams.items():
                qkv[name] = _inproj_even(xs[name], mod, layer, ctx_row, g_mix, w_t, gains, *rope[HEAD_DIM])
            cfg = _even_attention_cfg(lam_init)
            extras = [diff_lambda[i], _col(diff_subln[i] * (1.0 - lam_init))]
            extra_specs = [_const_spec((4, HEAD_DIM)), _const_spec((DIFF_V_DIM, 1))]
            w_out = ab_w_out[i]
        else:
            q_gain = mla_qk_norm[i, 0] * (MLA_SCALE * LOG2E)
            bound2 = MLA_QK_DIM ** 2 * _max_sq(q_gain) * _max_sq(mla_qk_norm[i, 1])
            weights = (mla_w_in[i].T.astype(BF16), _col(mla_q_norm[i]), mla_w_q_up[i].T.astype(BF16),
                       _col(mla_kv_norm[i]), mla_w_kv_up[i].T.astype(BF16),
                       _col(q_gain), _col(mla_qk_norm[i, 1]))
            for name, (ctx_row, rope) in streams.items():
                qkv[name] = _inproj_odd(xs[name], mod, layer, ctx_row, g_mix, *weights, *rope[MLA_ROPE])
            cfg = _odd_attention_cfg()
            extras, extra_specs = [], []
            w_out = mla_w_out[i]

        safe = (bound2 * NORM_MARGIN <= MAX_ABS_SCORE ** 2).astype(jnp.int32).reshape(1)
        o = {"lat": _attention(cfg, safe, qkv["lat"][0], [qkv["lat"][1], qkv["ctx"][1]],
                               [qkv["lat"][2], qkv["ctx"][2]], extras, extra_specs, QUERY_TILE,
                               "attention_lat")}
        if update_ctx:
            o["ctx"] = _attention(cfg, safe, qkv["ctx"][0], [qkv["ctx"][1]], [qkv["ctx"][2]], extras,
                                  extra_specs, n_ctx, "attention_ctx")
        wo, w1, w2 = w_out.astype(BF16), mlp_w1[layer].astype(BF16), mlp_w2[layer].astype(BF16)
        for name in o:
            xs[name] = _out_mlp(xs[name], o[name], mod, layer, streams[name][0],
                                norm_mlp[layer][None, :], wo, w1, w2)
    return xs["lat"]
```

```python
import functools
import math

import jax
import jax.numpy as jnp
from jax import lax
from jax.experimental import pallas as pl
from jax.experimental.pallas import tpu as pltpu

F32 = jnp.float32
BF16 = jnp.bfloat16

D_MODEL = 1024
DEPTH = 4
GRID_W = 64
D_FF = 4 * D_MODEL
N_MOD = 6
ROPE_THETA = 10000.0
RMS_EPS = 1e-6
LOG2E = math.log2(math.e)

HEAD_DIM = 64
DIFF_HEADS = 4
DIFF_V_DIM = 128
GQA_Q_HEADS = 8
GQA_KV_HEADS = 2
GQA_GROUP = GQA_Q_HEADS // GQA_KV_HEADS
AB_IN_W = 2304
QK_SCALE_64 = HEAD_DIM ** -0.5

MLA_NOPE = 64
MLA_ROPE = 32
MLA_QK_DIM = MLA_NOPE + MLA_ROPE
MLA_V = 64
MLA_HEADS = 16
MLA_Q_RANK = 512
MLA_KV_RANK = 256
MLA_IN_W = MLA_Q_RANK + MLA_KV_RANK + MLA_ROPE
MLA_SCALE = MLA_QK_DIM ** -0.5

BF16_ROWS = 16
ONES_ROWS = BF16_ROWS
EVEN_V_BLOCK = ONES_ROWS + DIFF_V_DIM
EVEN_V_ROWS = (DIFF_HEADS + GQA_KV_HEADS) * EVEN_V_BLOCK
ODD_V_BLOCK = ONES_ROWS + MLA_V
ODD_V_ROWS = MLA_HEADS * ODD_V_BLOCK
N_UNITS = 16
UNITS_PER_STEP = 4
FEW_KEYS = 512
KEY_CHUNK = 256
LOOKAHEAD = 3
FEW_KEYS_LOOKAHEAD = 8
MAX_ABS_SCORE = 60.0
NORM_MARGIN = 1.05

TOKEN_TILE = 1024
QUERY_TILE = 1024
EXACT_PATH_COLS = 512
MOD_ROWS = 40
FF_CHUNK = 1024
VMEM_LIMIT_BYTES = 56 * 1024 * 1024

_NT = (((1,), (1,)), ((), ()))
_TN = (((0,), (0,)), ((), ()))


def _lambda_init(layer):
    return 0.8 - 0.6 * math.exp(-0.3 * layer)


def _params(n_axes):
    return pltpu.CompilerParams(
        dimension_semantics=("arbitrary",) * n_axes, vmem_limit_bytes=VMEM_LIMIT_BYTES)


def _const_spec(shape):
    return pl.BlockSpec(shape, lambda *_: (0,) * len(shape), pipeline_mode=pl.Buffered(1))


def _rms_modulate(x, gain, shift, scale):
    ms = jnp.mean(x * x, axis=-1, keepdims=True)
    return (x * lax.rsqrt(ms + RMS_EPS) * gain) * (1.0 + scale) + shift


def _rms_rows(y, n):
    return lax.rsqrt(jnp.sum(y * y, axis=0, keepdims=True) * (1.0 / n) + RMS_EPS)


def _rope_rows(x1, x2, cos, sin):
    return x1 * cos - x2 * sin, x2 * cos + x1 * sin


def _ones_block(tokens):
    row = lax.broadcasted_iota(jnp.int32, (ONES_ROWS, tokens), 0)
    return (row == 0).astype(BF16)


def _mod_kernel(a_ref, w_ref, b_ref, o_ref):
    a = a_ref[...]
    a = a / (1.0 + jnp.exp(-a))
    o_ref[0] = jnp.dot(a.astype(BF16), w_ref[0].astype(BF16), preferred_element_type=F32) + b_ref[0]


def _modulation(c, c_ctx, ada_w, ada_b):
    b = c.shape[0]
    rows = jnp.concatenate([c, c_ctx[None, :], jnp.zeros((MOD_ROWS - b - 1, D_MODEL), F32)], axis=0)
    tn = 1024
    out = pl.pallas_call(
        _mod_kernel,
        grid=(DEPTH, N_MOD * D_MODEL // tn),
        in_specs=[
            pl.BlockSpec((MOD_ROWS, D_MODEL), lambda l, j: (0, 0)),
            pl.BlockSpec((1, D_MODEL, tn), lambda l, j: (l, 0, j)),
            pl.BlockSpec((1, 1, tn), lambda l, j: (l, 0, j)),
        ],
        out_specs=pl.BlockSpec((1, MOD_ROWS, tn), lambda l, j: (l, 0, j)),
        out_shape=jax.ShapeDtypeStruct((DEPTH, MOD_ROWS, N_MOD * D_MODEL), F32),
        compiler_params=_params(2),
        name="adaln_modulation",
    )(rows, ada_w, ada_b[:, None, :])
    return out.reshape(DEPTH, MOD_ROWS, N_MOD, D_MODEL)


def _mod_spec(layer, ctx_row):
    if ctx_row is None:
        return pl.BlockSpec((None, None, N_MOD, D_MODEL), lambda b, t: (layer, b, 0, 0))
    return pl.BlockSpec((None, None, N_MOD, D_MODEL), lambda b, t: (layer, ctx_row, 0, 0))


def _inproj_even_kernel(x_ref, mod_ref, g_ref, w_ref, gain_ref, cos_ref, sin_ref, q_ref, k_ref, v_ref):
    h = _rms_modulate(x_ref[0], g_ref[...], mod_ref[0:1, :], mod_ref[1:2, :]).astype(BF16)
    cos = cos_ref[...]
    sin = sin_ref[...]
    tokens = h.shape[0]

    def proj(r0, r1):
        return lax.dot_general(w_ref[r0:r1, :], h, _NT, preferred_element_type=F32)

    def qk_heads(y, r0, n_heads, dst_ref, d0):
        for i in range(n_heads):
            yh = y[i * HEAD_DIM:(i + 1) * HEAD_DIM]
            gcol = gain_ref[r0 + i * HEAD_DIM:r0 + (i + 1) * HEAD_DIM, :]
            yn = yh * _rms_rows(yh, HEAD_DIM) * gcol
            o1, o2 = _rope_rows(yn[0:32], yn[32:64], cos, sin)
            base = d0 + i * HEAD_DIM
            dst_ref[0, base:base + 32, :] = o1.astype(BF16)
            dst_ref[0, base + 32:base + 64, :] = o2.astype(BF16)

    ya = proj(0, 1024)
    qk_heads(ya[0:512], 0, 8, q_ref, 0)
    qk_heads(ya[512:1024], 512, 8, k_ref, 0)
    yb = proj(1536, 2304)
    qk_heads(yb[0:512], 1536, 8, q_ref, 512)
    qk_heads(yb[512:640], 2048, 2, k_ref, 512)
    ones = _ones_block(tokens)
    av = proj(1024, 1536).astype(BF16)
    for i in range(DIFF_HEADS):
        r = i * EVEN_V_BLOCK
        v_ref[0, r:r + ONES_ROWS, :] = ones
        v_ref[0, r + ONES_ROWS:r + EVEN_V_BLOCK, :] = av[i * DIFF_V_DIM:(i + 1) * DIFF_V_DIM]
    bv = yb[640:768].astype(BF16)
    for j in range(GQA_KV_HEADS):
        r = (DIFF_HEADS + j) * EVEN_V_BLOCK
        v_ref[0, r:r + ONES_ROWS, :] = ones
        v_ref[0, r + ONES_ROWS:r + ONES_ROWS + HEAD_DIM, :] = bv[j * HEAD_DIM:(j + 1) * HEAD_DIM]
        v_ref[0, r + ONES_ROWS + HEAD_DIM:r + EVEN_V_BLOCK, :] = jnp.zeros(
            (EVEN_V_BLOCK - ONES_ROWS - HEAD_DIM, tokens), BF16)


def _inproj_even(x, mod, layer, ctx_row, g, w_t, gains, cos_t, sin_t):
    b, n, _ = x.shape
    tm = min(TOKEN_TILE, n)
    tok = lambda rows: pl.BlockSpec((1, rows, tm), lambda i, t: (i, 0, t))
    rope = pl.BlockSpec((32, tm), lambda i, t: (0, t))
    return pl.pallas_call(
        _inproj_even_kernel,
        grid=(b, n // tm),
        in_specs=[
            pl.BlockSpec((1, tm, D_MODEL), lambda i, t: (i, t, 0)),
            _mod_spec(layer, ctx_row),
            _const_spec((1, D_MODEL)),
            _const_spec((AB_IN_W, D_MODEL)),
            _const_spec((AB_IN_W, 1)),
            rope, rope,
        ],
        out_specs=[tok(1024), tok(640), tok(EVEN_V_ROWS)],
        out_shape=[jax.ShapeDtypeStruct((b, 1024, n), BF16),
                   jax.ShapeDtypeStruct((b, 640, n), BF16),
                   jax.ShapeDtypeStruct((b, EVEN_V_ROWS, n), BF16)],
        compiler_params=_params(2),
        name="inproj_even",
    )(x, mod, g, w_t, gains, cos_t, sin_t)


def _inproj_odd_kernel(x_ref, mod_ref, g_ref, w_ref, qn_ref, wq_ref, kvn_ref, wkv_ref, qg_ref, kg_ref,
                       cos_ref, sin_ref, q_ref, k_ref, v_ref):
    h = _rms_modulate(x_ref[0], g_ref[...], mod_ref[0:1, :], mod_ref[1:2, :]).astype(BF16)
    cos = cos_ref[...]
    sin = sin_ref[...]
    y = lax.dot_general(w_ref[...], h, _NT, preferred_element_type=F32)
    qc = y[0:MLA_Q_RANK]
    kvc = y[MLA_Q_RANK:MLA_Q_RANK + MLA_KV_RANK]
    kr = y[MLA_Q_RANK + MLA_KV_RANK:MLA_IN_W]
    qcn = (qc * _rms_rows(qc, MLA_Q_RANK) * qn_ref[...]).astype(BF16)
    kvcn = (kvc * _rms_rows(kvc, MLA_KV_RANK) * kvn_ref[...]).astype(BF16)
    q = jnp.dot(wq_ref[...], qcn, preferred_element_type=F32)
    kv = jnp.dot(wkv_ref[...], kvcn, preferred_element_type=F32)
    kr_ss = jnp.sum(kr * kr, axis=0, keepdims=True)
    qg = qg_ref[...]
    kg = kg_ref[...]
    ones = _ones_block(h.shape[0])
    for i in range(MLA_HEADS):
        b0 = i * MLA_QK_DIM
        qh = q[b0:b0 + MLA_QK_DIM]
        qn = qh * _rms_rows(qh, MLA_QK_DIM) * qg
        o1, o2 = _rope_rows(qn[64:80], qn[80:96], cos, sin)
        q_ref[0, b0:b0 + 64, :] = qn[0:64].astype(BF16)
        q_ref[0, b0 + 64:b0 + 80, :] = o1.astype(BF16)
        q_ref[0, b0 + 80:b0 + 96, :] = o2.astype(BF16)
        kn = kv[i * 128:i * 128 + MLA_NOPE]
        r = lax.rsqrt((jnp.sum(kn * kn, axis=0, keepdims=True) + kr_ss) * (1.0 / MLA_QK_DIM) + RMS_EPS)
        krn = kr * r * kg[64:96]
        o1, o2 = _rope_rows(krn[0:16], krn[16:32], cos, sin)
        k_ref[0, b0:b0 + 64, :] = (kn * r * kg[0:64]).astype(BF16)
        k_ref[0, b0 + 64:b0 + 80, :] = o1.astype(BF16)
        k_ref[0, b0 + 80:b0 + 96, :] = o2.astype(BF16)
        v0 = i * ODD_V_BLOCK
        v_ref[0, v0:v0 + ONES_ROWS, :] = ones
        v_ref[0, v0 + ONES_ROWS:v0 + ODD_V_BLOCK, :] = kv[i * 128 + MLA_NOPE:(i + 1) * 128].astype(BF16)


def _inproj_odd(x, mod, layer, ctx_row, g, w_t, qn, wq_t, kvn, wkv_t, qg, kg, cos_t, sin_t):
    b, n, _ = x.shape
    tm = min(TOKEN_TILE, n)
    tok = lambda rows: pl.BlockSpec((1, rows, tm), lambda i, t: (i, 0, t))
    rope = pl.BlockSpec((16, tm), lambda i, t: (0, t))
    nq = MLA_HEADS * MLA_QK_DIM
    return pl.pallas_call(
        _inproj_odd_kernel,
        grid=(b, n // tm),
        in_specs=[
            pl.BlockSpec((1, tm, D_MODEL), lambda i, t: (i, t, 0)),
            _mod_spec(layer, ctx_row),
            _const_spec((1, D_MODEL)),
            _const_spec((MLA_IN_W, D_MODEL)),
            _const_spec((MLA_Q_RANK, 1)),
            _const_spec((nq, MLA_Q_RANK)),
            _const_spec((MLA_KV_RANK, 1)),
            _const_spec((MLA_HEADS * (MLA_NOPE + MLA_V), MLA_KV_RANK)),
            _const_spec((MLA_QK_DIM, 1)),
            _const_spec((MLA_QK_DIM, 1)),
            rope, rope,
        ],
        out_specs=[tok(nq), tok(nq), tok(ODD_V_ROWS)],
        out_shape=[jax.ShapeDtypeStruct((b, nq, n), BF16),
                   jax.ShapeDtypeStruct((b, nq, n), BF16),
                   jax.ShapeDtypeStruct((b, ODD_V_ROWS, n), BF16)],
        compiler_params=_params(2),
        name="inproj_odd",
    )(x, mod, g, w_t, qn, wq_t, kvn, wkv_t, qg, kg, cos_t, sin_t)


def _attention_kernel(cfg, n_kv, safe_ref, *refs):
    d, v_rows, dv, q_row, k_row, v_row, _, emit, finish = cfg
    q_ref = refs[0]
    k_refs = refs[1:1 + n_kv]
    v_refs = refs[1 + n_kv:1 + 2 * n_kv]
    *extra, o_ref, s0, s1, m0, m1, stage = refs[1 + 2 * n_kv:]
    s_bufs, m_bufs = (s0, s1), (m0, m1)
    bounds = [0]
    for k_ref in k_refs:
        bounds.append(bounds[-1] + k_ref.shape[2])
    tq = q_ref.shape[2]
    cols = s0.shape[1]

    def q_block(i, c0, width):
        return q_ref[0, pl.ds(pl.multiple_of(q_row(i), BF16_ROWS), d), c0:c0 + width]

    def normalised(ox):
        return ox[ONES_ROWS:ONES_ROWS + dv] * (1.0 / ox[0:1])

    def shift_free_units(first, count, lookahead):
        tasks = []
        for i in (first + u for u in range(count)):
            q = q_block(i, 0, tq)
            kr = pl.multiple_of(k_row(i), BF16_ROWS)
            vr = pl.multiple_of(v_row(i), BF16_ROWS)
            for k_ref, v_ref in zip(k_refs, v_refs):
                for k0 in range(0, k_ref.shape[2], KEY_CHUNK):
                    ks = slice(k0, min(k0 + KEY_CHUNK, k_ref.shape[2]))
                    tasks.append((i, q, k_ref.at[0, pl.ds(kr, d), ks], v_ref.at[0, pl.ds(vr, v_rows), ks]))
        per_unit = len(tasks) // count
        scores, acc = {}, None
        for n in range(len(tasks) + lookahead):
            if n < len(tasks):
                _, q, k_blk, _ = tasks[n]
                scores[n] = lax.dot_general(k_blk[...], q, _TN, preferred_element_type=F32)
            n_val = n - lookahead
            if n_val >= 0:
                i, _, _, v_blk = tasks[n_val]
                oj = jnp.dot(v_blk[...], jnp.exp2(scores.pop(n_val)).astype(BF16),
                             preferred_element_type=F32)
                acc = oj if n_val % per_unit == 0 else acc + oj
                if n_val % per_unit == per_unit - 1:
                    emit(i, normalised(acc), 0, tq, stage, o_ref)

    def scores(i, par, c0):
        q = q_block(i, c0, cols)
        kr = pl.multiple_of(k_row(i), BF16_ROWS)
        m = None
        for j, k_ref in enumerate(k_refs):
            s = lax.dot_general(k_ref[0, pl.ds(kr, d), :], q, _TN, preferred_element_type=F32)
            s_bufs[par][bounds[j]:bounds[j + 1], :] = s
            mj = jnp.max(s, axis=0, keepdims=True)
            m = mj if m is None else jnp.maximum(m, mj)
        m_bufs[par][...] = jnp.broadcast_to(m, m_bufs[par].shape)

    def values(i, par, c0):
        m = m_bufs[par][0:1, :]
        vr = pl.multiple_of(v_row(i), BF16_ROWS)
        ox = None
        for j, v_ref in enumerate(v_refs):
            p = jnp.exp2(s_bufs[par][bounds[j]:bounds[j + 1], :] - m).astype(BF16)
            oj = jnp.dot(v_ref[0, pl.ds(vr, v_rows), :], p, preferred_element_type=F32)
            ox = oj if ox is None else ox + oj
        emit(i, normalised(ox), c0, cols, stage, o_ref)

    @pl.when(safe_ref[0] != 0)
    def _():
        if bounds[-1] <= FEW_KEYS:
            shift_free_units(0, N_UNITS, FEW_KEYS_LOOKAHEAD)
        else:
            def step(t, carry):
                shift_free_units(t * UNITS_PER_STEP, UNITS_PER_STEP, LOOKAHEAD)
                return carry

            lax.fori_loop(0, N_UNITS // UNITS_PER_STEP, step, 0)

    @pl.when(safe_ref[0] == 0)
    def _():
        for c0 in range(0, tq, cols):
            scores(0, 0, c0)

            def step(t, carry):
                @pl.when(t % 2 == 0)
                def _():
                    scores(t + 1, 1, c0)
                    values(t, 0, c0)

                @pl.when(t % 2 == 1)
                def _():
                    scores(t + 1, 0, c0)
                    values(t, 1, c0)
                return carry

            lax.fori_loop(0, N_UNITS - 1, step, 0)
            values(N_UNITS - 1, (N_UNITS - 1) % 2, c0)

    finish(stage, extra, o_ref)


def _attention(cfg, safe, q_t, k_ts, v_ts, extras, extra_specs, tq, name):
    b, rq, nq = q_t.shape
    stage_shape = cfg[6](tq)
    cols = min(tq, EXACT_PATH_COLS)
    nk = sum(k.shape[2] for k in k_ts)
    tile = lambda rows: pl.BlockSpec((1, rows, tq), lambda i, t, safe_ref: (i, 0, t))
    full = lambda a: pl.BlockSpec((1,) + a.shape[1:], lambda i, t, safe_ref: (i, 0, 0))
    return pl.pallas_call(
        functools.partial(_attention_kernel, cfg, len(k_ts)),
        grid_spec=pltpu.PrefetchScalarGridSpec(
            num_scalar_prefetch=1,
            grid=(b, nq // tq),
            in_specs=[tile(rq)] + [full(a) for a in k_ts] + [full(a) for a in v_ts] + extra_specs,
            out_specs=tile(D_MODEL),
            scratch_shapes=[pltpu.VMEM((nk, cols), F32), pltpu.VMEM((nk, cols), F32),
                            pltpu.VMEM((8, cols), F32), pltpu.VMEM((8, cols), F32),
                            pltpu.VMEM(stage_shape, F32)]),
        out_shape=jax.ShapeDtypeStruct((b, D_MODEL, nq), BF16),
        compiler_params=_params(2),
        name=name,
    )(safe, q_t, *k_ts, *v_ts, *extras)


def _even_attention_cfg(lam_init):
    def k_row(i):
        return jnp.where(i < 8, i * HEAD_DIM, 512 + ((i - 8) // GQA_GROUP) * HEAD_DIM)

    def v_row(i):
        return jnp.where(i < 8, i // 2, DIFF_HEADS + (i - 8) // GQA_GROUP) * EVEN_V_BLOCK

    def emit(i, o, c0, width, stage, o_ref):
        stage[i, :, c0:c0 + width] = o

    def finish(stage, extra, o_ref):
        dl_ref, sub_ref = extra
        dl = dl_ref[...]
        lam = (jnp.exp(jnp.sum(dl[0:1] * dl[1:2], axis=1, keepdims=True))
               - jnp.exp(jnp.sum(dl[2:3] * dl[3:4], axis=1, keepdims=True)) + lam_init)
        for h in range(DIFF_HEADS):
            o = stage[2 * h] - lam * stage[2 * h + 1]
            o = o * _rms_rows(o, DIFF_V_DIM) * sub_ref[...]
            o_ref[0, h * DIFF_V_DIM:(h + 1) * DIFF_V_DIM, :] = o.astype(BF16)
        for g in range(GQA_Q_HEADS):
            r = 512 + g * HEAD_DIM
            o_ref[0, r:r + HEAD_DIM, :] = stage[8 + g, 0:HEAD_DIM, :].astype(BF16)

    return (HEAD_DIM, EVEN_V_BLOCK, DIFF_V_DIM, lambda i: i * HEAD_DIM, k_row, v_row,
            lambda tq: (N_UNITS, DIFF_V_DIM, tq), emit, finish)


def _odd_attention_cfg():
    def emit(i, o, c0, width, stage, o_ref):
        o_ref[0, pl.ds(pl.multiple_of(i * MLA_V, BF16_ROWS), MLA_V), c0:c0 + width] = o.astype(BF16)

    def finish(stage, extra, o_ref):
        pass

    return (MLA_QK_DIM, ODD_V_BLOCK, MLA_V, lambda i: i * MLA_QK_DIM, lambda i: i * MLA_QK_DIM,
            lambda i: i * ODD_V_BLOCK, lambda tq: (8, 128), emit, finish)


def _out_mlp_kernel(x_ref, o_ref, mod_ref, g_ref, wo_ref, w1_ref, w2_ref, out_ref):
    y = lax.dot_general(o_ref[0], wo_ref[...], _TN, preferred_element_type=F32)
    x1 = x_ref[0] + mod_ref[2:3, :] * y
    h = _rms_modulate(x1, g_ref[...], mod_ref[3:4, :], mod_ref[4:5, :]).astype(BF16)
    acc = jnp.zeros(x1.shape, F32)
    for c in range(D_FF // FF_CHUNK):
        u = jnp.dot(h, w1_ref[:, c * FF_CHUNK:(c + 1) * FF_CHUNK], preferred_element_type=F32)
        a = jnp.square(jnp.maximum(u, 0.0)).astype(BF16)
        acc = acc + jnp.dot(a, w2_ref[c * FF_CHUNK:(c + 1) * FF_CHUNK, :], preferred_element_type=F32)
    out_ref[0] = x1 + mod_ref[5:6, :] * acc


def _out_mlp(x, o_t, mod, layer, ctx_row, g, wo, w1, w2):
    b, n, _ = x.shape
    tm = min(TOKEN_TILE, n)
    return pl.pallas_call(
        _out_mlp_kernel,
        grid=(b, n // tm),
        in_specs=[
            pl.BlockSpec((1, tm, D_MODEL), lambda i, t: (i, t, 0)),
            pl.BlockSpec((1, D_MODEL, tm), lambda i, t: (i, 0, t)),
            _mod_spec(layer, ctx_row),
            _const_spec((1, D_MODEL)),
            _const_spec((D_MODEL, D_MODEL)),
            _const_spec((D_MODEL, D_FF)),
            _const_spec((D_FF, D_MODEL)),
        ],
        out_specs=pl.BlockSpec((1, tm, D_MODEL), lambda i, t: (i, t, 0)),
        out_shape=jax.ShapeDtypeStruct((b, n, D_MODEL), F32),
        compiler_params=_params(2),
        name="outproj_mlp",
    )(x, o_t, mod, g, wo, w1, w2)


def _rope_tables(n_lat, rot_dim):
    rows = n_lat // GRID_W
    n_freq = rot_dim // 4
    inv_freq = ROPE_THETA ** (-jnp.arange(n_freq, dtype=F32) / n_freq)
    row = jnp.repeat(jnp.arange(rows, dtype=F32), GRID_W)
    col = jnp.tile(jnp.arange(GRID_W, dtype=F32), rows)
    ang = jnp.concatenate([row[:, None] * inv_freq, col[:, None] * inv_freq], axis=-1)
    return jnp.cos(ang).T, jnp.sin(ang).T


def _identity_rope(n_tokens, rot_dim):
    return jnp.ones((rot_dim // 2, n_tokens), F32), jnp.zeros((rot_dim // 2, n_tokens), F32)


def _col(v):
    return v.astype(F32)[:, None]


def _max_sq(gain):
    return jnp.max(jnp.square(gain.astype(F32)))


def kernel(x, c, ctx, c_ctx, ada_w, ada_b, norm_mix, norm_mlp, mlp_w1, mlp_w2, ab_w_in, ab_w_out,
           diff_qk_norm, diff_lambda, diff_subln, gqa_qk_norm, mla_w_in, mla_q_norm, mla_w_q_up,
           mla_kv_norm, mla_w_kv_up, mla_qk_norm, mla_w_out):
    b, n_lat, d = x.shape
    n_ctx = ctx.shape[1]
    assert d == D_MODEL and b < MOD_ROWS
    assert n_lat % QUERY_TILE == 0 and n_lat % TOKEN_TILE == 0 and n_ctx <= TOKEN_TILE

    mod = _modulation(c, c_ctx, ada_w, ada_b)
    rope_lat = {HEAD_DIM: _rope_tables(n_lat, HEAD_DIM), MLA_ROPE: _rope_tables(n_lat, MLA_ROPE)}
    rope_ctx = {HEAD_DIM: _identity_rope(n_ctx, HEAD_DIM), MLA_ROPE: _identity_rope(n_ctx, MLA_ROPE)}
    streams = {"lat": (None, rope_lat), "ctx": (b, rope_ctx)}
    xs = {"lat": x, "ctx": ctx}

    for layer in range(DEPTH):
        update_ctx = layer < DEPTH - 1
        i = layer // 2
        g_mix = norm_mix[layer][None, :]
        qkv = {}
        if layer % 2 == 0:
            lam_init = _lambda_init(layer)
            ones = jnp.ones((HEAD_DIM,), F32)
            q_scale = QK_SCALE_64 * LOG2E
            gains = _col(jnp.concatenate([
                jnp.tile(diff_qk_norm[i, 0] * q_scale, 8), jnp.tile(diff_qk_norm[i, 1], 8),
                jnp.tile(ones, 8),
                jnp.tile(gqa_qk_norm[i, 0] * q_scale, 8), jnp.tile(gqa_qk_norm[i, 1], 2),
                jnp.tile(ones, 2)]))
            bound2 = HEAD_DIM ** 2 * jnp.maximum(
                _max_sq(diff_qk_norm[i, 0] * q_scale) * _max_sq(diff_qk_norm[i, 1]),
                _max_sq(gqa_qk_norm[i, 0] * q_scale) * _max_sq(gqa_qk_norm[i, 1]))
            w_t = ab_w_in[i].T.astype(BF16)
            for name, (ctx_row, rope) in streams.items():
                qkv[name] = _inproj_even(xs[name], mod, layer, ctx_row, g_mix, w_t, gains, *rope[HEAD_DIM])
            cfg = _even_attention_cfg(lam_init)
            extras = [diff_lambda[i], _col(diff_subln[i] * (1.0 - lam_init))]
            extra_specs = [_const_spec((4, HEAD_DIM)), _const_spec((DIFF_V_DIM, 1))]
            w_out = ab_w_out[i]
        else:
            q_gain = mla_qk_norm[i, 0] * (MLA_SCALE * LOG2E)
            bound2 = MLA_QK_DIM ** 2 * _max_sq(q_gain) * _max_sq(mla_qk_norm[i, 1])
            weights = (mla_w_in[i].T.astype(BF16), _col(mla_q_norm[i]), mla_w_q_up[i].T.astype(BF16),
                       _col(mla_kv_norm[i]), mla_w_kv_up[i].T.astype(BF16),
                       _col(q_gain), _col(mla_qk_norm[i, 1]))
            for name, (ctx_row, rope) in streams.items():
                qkv[name] = _inproj_odd(xs[name], mod, layer, ctx_row, g_mix, *weights, *rope[MLA_ROPE])
            cfg = _odd_attention_cfg()
            extras, extra_specs = [], []
            w_out = mla_w_out[i]

        safe = (bound2 * NORM_MARGIN <= MAX_ABS_SCORE ** 2).astype(jnp.int32).reshape(1)
        o = {"lat": _attention(cfg, safe, qkv["lat"][0], [qkv["lat"][1], qkv["ctx"][1]],
                               [qkv["lat"][2], qkv["ctx"][2]], extras, extra_specs, QUERY_TILE,
                               "attention_lat")}
        if update_ctx:
            o["ctx"] = _attention(cfg, safe, qkv["ctx"][0], [qkv["ctx"][1]], [qkv["ctx"][2]], extras,
                                  extra_specs, n_ctx, "attention_ctx")
        wo, w1, w2 = w_out.astype(BF16), mlp_w1[layer].astype(BF16), mlp_w2[layer].astype(BF16)
        for name in o:
            xs[name] = _out_mlp(xs[name], o[name], mod, layer, streams[name][0],
                                norm_mlp[layer][None, :], wo, w1, w2)
    return xs["lat"]
```

```python
import functools
import math

import jax
import jax.numpy as jnp
from jax import lax
from jax.experimental import pallas as pl
from jax.experimental.pallas import tpu as pltpu

F32 = jnp.float32
BF16 = jnp.bfloat16

D_MODEL = 1024
DEPTH = 4
GRID_W = 64
D_FF = 4 * D_MODEL
N_MOD = 6
ROPE_THETA = 10000.0
RMS_EPS = 1e-6
LOG2E = math.log2(math.e)

HEAD_DIM = 64
DIFF_HEADS = 4
DIFF_V_DIM = 128
GQA_Q_HEADS = 8
GQA_KV_HEADS = 2
GQA_GROUP = GQA_Q_HEADS // GQA_KV_HEADS
AB_IN_W = 2304
QK_SCALE_64 = HEAD_DIM ** -0.5

MLA_NOPE = 64
MLA_ROPE = 32
MLA_QK_DIM = MLA_NOPE + MLA_ROPE
MLA_V = 64
MLA_HEADS = 16
MLA_Q_RANK = 512
MLA_KV_RANK = 256
MLA_IN_W = MLA_Q_RANK + MLA_KV_RANK + MLA_ROPE
MLA_SCALE = MLA_QK_DIM ** -0.5

BF16_ROWS = 16
ONES_ROWS = BF16_ROWS
EVEN_V_BLOCK = ONES_ROWS + DIFF_V_DIM
EVEN_V_ROWS = (DIFF_HEADS + GQA_KV_HEADS) * EVEN_V_BLOCK
ODD_V_BLOCK = ONES_ROWS + MLA_V
ODD_V_ROWS = MLA_HEADS * ODD_V_BLOCK
N_UNITS = 16
UNITS_PER_STEP = 8
FEW_KEYS = 512
KEY_CHUNK = 256
LOOKAHEAD = 3
FEW_KEYS_LOOKAHEAD = 8
MAX_ABS_SCORE = 60.0
NORM_MARGIN = 1.05

TOKEN_TILE = 1024
QUERY_TILE = 1024
EXACT_PATH_COLS = 512
MOD_ROWS = 40
FF_CHUNK = 1024
VMEM_LIMIT_BYTES = 56 * 1024 * 1024

_NT = (((1,), (1,)), ((), ()))
_TN = (((0,), (0,)), ((), ()))


def _lambda_init(layer):
    return 0.8 - 0.6 * math.exp(-0.3 * layer)


def _params(n_axes):
    return pltpu.CompilerParams(
        dimension_semantics=("arbitrary",) * n_axes, vmem_limit_bytes=VMEM_LIMIT_BYTES)


def _const_spec(shape):
    return pl.BlockSpec(shape, lambda *_: (0,) * len(shape), pipeline_mode=pl.Buffered(1))


def _rms_modulate(x, gain, shift, scale):
    ms = jnp.mean(x * x, axis=-1, keepdims=True)
    return (x * lax.rsqrt(ms + RMS_EPS) * gain) * (1.0 + scale) + shift


def _rms_rows(y, n):
    return lax.rsqrt(jnp.sum(y * y, axis=0, keepdims=True) * (1.0 / n) + RMS_EPS)


def _rope_rows(x1, x2, cos, sin):
    return x1 * cos - x2 * sin, x2 * cos + x1 * sin


def _ones_block(tokens):
    row = lax.broadcasted_iota(jnp.int32, (ONES_ROWS, tokens), 0)
    return (row == 0).astype(BF16)


def _mod_kernel(a_ref, w_ref, b_ref, o_ref):
    a = a_ref[...]
    a = a / (1.0 + jnp.exp(-a))
    o_ref[0] = jnp.dot(a.astype(BF16), w_ref[0].astype(BF16), preferred_element_type=F32) + b_ref[0]


def _modulation(c, c_ctx, ada_w, ada_b):
    b = c.shape[0]
    rows = jnp.concatenate([c, c_ctx[None, :], jnp.zeros((MOD_ROWS - b - 1, D_MODEL), F32)], axis=0)
    tn = 1024
    out = pl.pallas_call(
        _mod_kernel,
        grid=(DEPTH, N_MOD * D_MODEL // tn),
        in_specs=[
            pl.BlockSpec((MOD_ROWS, D_MODEL), lambda l, j: (0, 0)),
            pl.BlockSpec((1, D_MODEL, tn), lambda l, j: (l, 0, j)),
            pl.BlockSpec((1, 1, tn), lambda l, j: (l, 0, j)),
        ],
        out_specs=pl.BlockSpec((1, MOD_ROWS, tn), lambda l, j: (l, 0, j)),
        out_shape=jax.ShapeDtypeStruct((DEPTH, MOD_ROWS, N_MOD * D_MODEL), F32),
        compiler_params=_params(2),
        name="adaln_modulation",
    )(rows, ada_w, ada_b[:, None, :])
    return out.reshape(DEPTH, MOD_ROWS, N_MOD, D_MODEL)


def _mod_spec(layer, ctx_row):
    if ctx_row is None:
        return pl.BlockSpec((None, None, N_MOD, D_MODEL), lambda b, t: (layer, b, 0, 0))
    return pl.BlockSpec((None, None, N_MOD, D_MODEL), lambda b, t: (layer, ctx_row, 0, 0))


def _inproj_even_kernel(x_ref, mod_ref, g_ref, w_ref, gain_ref, cos_ref, sin_ref, q_ref, k_ref, v_ref):
    h = _rms_modulate(x_ref[0], g_ref[...], mod_ref[0:1, :], mod_ref[1:2, :]).astype(BF16)
    cos = cos_ref[...]
    sin = sin_ref[...]
    tokens = h.shape[0]

    def proj(r0, r1):
        return lax.dot_general(w_ref[r0:r1, :], h, _NT, preferred_element_type=F32)

    def qk_heads(y, r0, n_heads, dst_ref, d0):
        for i in range(n_heads):
            yh = y[i * HEAD_DIM:(i + 1) * HEAD_DIM]
            gcol = gain_ref[r0 + i * HEAD_DIM:r0 + (i + 1) * HEAD_DIM, :]
            yn = yh * _rms_rows(yh, HEAD_DIM) * gcol
            o1, o2 = _rope_rows(yn[0:32], yn[32:64], cos, sin)
            base = d0 + i * HEAD_DIM
            dst_ref[0, base:base + 32, :] = o1.astype(BF16)
            dst_ref[0, base + 32:base + 64, :] = o2.astype(BF16)

    ya = proj(0, 1024)
    qk_heads(ya[0:512], 0, 8, q_ref, 0)
    qk_heads(ya[512:1024], 512, 8, k_ref, 0)
    yb = proj(1536, 2304)
    qk_heads(yb[0:512], 1536, 8, q_ref, 512)
    qk_heads(yb[512:640], 2048, 2, k_ref, 512)
    ones = _ones_block(tokens)
    av = proj(1024, 1536).astype(BF16)
    for i in range(DIFF_HEADS):
        r = i * EVEN_V_BLOCK
        v_ref[0, r:r + ONES_ROWS, :] = ones
        v_ref[0, r + ONES_ROWS:r + EVEN_V_BLOCK, :] = av[i * DIFF_V_DIM:(i + 1) * DIFF_V_DIM]
    bv = yb[640:768].astype(BF16)
    for j in range(GQA_KV_HEADS):
        r = (DIFF_HEADS + j) * EVEN_V_BLOCK
        v_ref[0, r:r + ONES_ROWS, :] = ones
        v_ref[0, r + ONES_ROWS:r + ONES_ROWS + HEAD_DIM, :] = bv[j * HEAD_DIM:(j + 1) * HEAD_DIM]
        v_ref[0, r + ONES_ROWS + HEAD_DIM:r + EVEN_V_BLOCK, :] = jnp.zeros(
            (EVEN_V_BLOCK - ONES_ROWS - HEAD_DIM, tokens), BF16)


def _inproj_even(x, mod, layer, ctx_row, g, w_t, gains, cos_t, sin_t):
    b, n, _ = x.shape
    tm = min(TOKEN_TILE, n)
    tok = lambda rows: pl.BlockSpec((1, rows, tm), lambda i, t: (i, 0, t))
    rope = pl.BlockSpec((32, tm), lambda i, t: (0, t))
    return pl.pallas_call(
        _inproj_even_kernel,
        grid=(b, n // tm),
        in_specs=[
            pl.BlockSpec((1, tm, D_MODEL), lambda i, t: (i, t, 0)),
            _mod_spec(layer, ctx_row),
            _const_spec((1, D_MODEL)),
            _const_spec((AB_IN_W, D_MODEL)),
            _const_spec((AB_IN_W, 1)),
            rope, rope,
        ],
        out_specs=[tok(1024), tok(640), tok(EVEN_V_ROWS)],
        out_shape=[jax.ShapeDtypeStruct((b, 1024, n), BF16),
                   jax.ShapeDtypeStruct((b, 640, n), BF16),
                   jax.ShapeDtypeStruct((b, EVEN_V_ROWS, n), BF16)],
        compiler_params=_params(2),
        name="inproj_even",
    )(x, mod, g, w_t, gains, cos_t, sin_t)


def _inproj_odd_kernel(x_ref, mod_ref, g_ref, w_ref, qn_ref, wq_ref, kvn_ref, wkv_ref, qg_ref, kg_ref,
                       cos_ref, sin_ref, q_ref, k_ref, v_ref):
    h = _rms_modulate(x_ref[0], g_ref[...], mod_ref[0:1, :], mod_ref[1:2, :]).astype(BF16)
    cos = cos_ref[...]
    sin = sin_ref[...]
    y = lax.dot_general(w_ref[...], h, _NT, preferred_element_type=F32)
    qc = y[0:MLA_Q_RANK]
    kvc = y[MLA_Q_RANK:MLA_Q_RANK + MLA_KV_RANK]
    kr = y[MLA_Q_RANK + MLA_KV_RANK:MLA_IN_W]
    qcn = (qc * _rms_rows(qc, MLA_Q_RANK) * qn_ref[...]).astype(BF16)
    kvcn = (kvc * _rms_rows(kvc, MLA_KV_RANK) * kvn_ref[...]).astype(BF16)
    q = jnp.dot(wq_ref[...], qcn, preferred_element_type=F32)
    kv = jnp.dot(wkv_ref[...], kvcn, preferred_element_type=F32)
    kr_ss = jnp.sum(kr * kr, axis=0, keepdims=True)
    qg = qg_ref[...]
    kg = kg_ref[...]
    ones = _ones_block(h.shape[0])
    for i in range(MLA_HEADS):
        b0 = i * MLA_QK_DIM
        qh = q[b0:b0 + MLA_QK_DIM]
        qn = qh * _rms_rows(qh, MLA_QK_DIM) * qg
        o1, o2 = _rope_rows(qn[64:80], qn[80:96], cos, sin)
        q_ref[0, b0:b0 + 64, :] = qn[0:64].astype(BF16)
        q_ref[0, b0 + 64:b0 + 80, :] = o1.astype(BF16)
        q_ref[0, b0 + 80:b0 + 96, :] = o2.astype(BF16)
        kn = kv[i * 128:i * 128 + MLA_NOPE]
        r = lax.rsqrt((jnp.sum(kn * kn, axis=0, keepdims=True) + kr_ss) * (1.0 / MLA_QK_DIM) + RMS_EPS)
        krn = kr * r * kg[64:96]
        o1, o2 = _rope_rows(krn[0:16], krn[16:32], cos, sin)
        k_ref[0, b0:b0 + 64, :] = (kn * r * kg[0:64]).astype(BF16)
        k_ref[0, b0 + 64:b0 + 80, :] = o1.astype(BF16)
        k_ref[0, b0 + 80:b0 + 96, :] = o2.astype(BF16)
        v0 = i * ODD_V_BLOCK
        v_ref[0, v0:v0 + ONES_ROWS, :] = ones
        v_ref[0, v0 + ONES_ROWS:v0 + ODD_V_BLOCK, :] = kv[i * 128 + MLA_NOPE:(i + 1) * 128].astype(BF16)


def _inproj_odd(x, mod, layer, ctx_row, g, w_t, qn, wq_t, kvn, wkv_t, qg, kg, cos_t, sin_t):
    b, n, _ = x.shape
    tm = min(TOKEN_TILE, n)
    tok = lambda rows: pl.BlockSpec((1, rows, tm), lambda i, t: (i, 0, t))
    rope = pl.BlockSpec((16, tm), lambda i, t: (0, t))
    nq = MLA_HEADS * MLA_QK_DIM
    return pl.pallas_call(
        _inproj_odd_kernel,
        grid=(b, n // tm),
        in_specs=[
            pl.BlockSpec((1, tm, D_MODEL), lambda i, t: (i, t, 0)),
            _mod_spec(layer, ctx_row),
            _const_spec((1, D_MODEL)),
            _const_spec((MLA_IN_W, D_MODEL)),
            _const_spec((MLA_Q_RANK, 1)),
            _const_spec((nq, MLA_Q_RANK)),
            _const_spec((MLA_KV_RANK, 1)),
            _const_spec((MLA_HEADS * (MLA_NOPE + MLA_V), MLA_KV_RANK)),
            _const_spec((MLA_QK_DIM, 1)),
            _const_spec((MLA_QK_DIM, 1)),
            rope, rope,
        ],
        out_specs=[tok(nq), tok(nq), tok(ODD_V_ROWS)],
        out_shape=[jax.ShapeDtypeStruct((b, nq, n), BF16),
                   jax.ShapeDtypeStruct((b, nq, n), BF16),
                   jax.ShapeDtypeStruct((b, ODD_V_ROWS, n), BF16)],
        compiler_params=_params(2),
        name="inproj_odd",
    )(x, mod, g, w_t, qn, wq_t, kvn, wkv_t, qg, kg, cos_t, sin_t)


def _attention_kernel(cfg, n_kv, safe_ref, *refs):
    d, v_rows, dv, q_row, k_row, v_row, _, emit, finish = cfg
    q_ref = refs[0]
    k_refs = refs[1:1 + n_kv]
    v_refs = refs[1 + n_kv:1 + 2 * n_kv]
    *extra, o_ref, s0, s1, m0, m1, stage = refs[1 + 2 * n_kv:]
    s_bufs, m_bufs = (s0, s1), (m0, m1)
    bounds = [0]
    for k_ref in k_refs:
        bounds.append(bounds[-1] + k_ref.shape[2])
    tq = q_ref.shape[2]
    cols = s0.shape[1]

    def q_block(i, c0, width):
        return q_ref[0, pl.ds(pl.multiple_of(q_row(i), BF16_ROWS), d), c0:c0 + width]

    def normalised(ox):
        return ox[ONES_ROWS:ONES_ROWS + dv] * (1.0 / ox[0:1])

    def shift_free_units(first, count, lookahead):
        tasks = []
        for i in (first + u for u in range(count)):
            q = q_block(i, 0, tq)
            kr = pl.multiple_of(k_row(i), BF16_ROWS)
            vr = pl.multiple_of(v_row(i), BF16_ROWS)
            for k_ref, v_ref in zip(k_refs, v_refs):
                for k0 in range(0, k_ref.shape[2], KEY_CHUNK):
                    ks = slice(k0, min(k0 + KEY_CHUNK, k_ref.shape[2]))
                    tasks.append((i, q, k_ref.at[0, pl.ds(kr, d), ks], v_ref.at[0, pl.ds(vr, v_rows), ks]))
        per_unit = len(tasks) // count
        scores, acc = {}, None
        for n in range(len(tasks) + lookahead):
            if n < len(tasks):
                _, q, k_blk, _ = tasks[n]
                scores[n] = lax.dot_general(k_blk[...], q, _TN, preferred_element_type=F32)
            n_val = n - lookahead
            if n_val >= 0:
                i, _, _, v_blk = tasks[n_val]
                oj = jnp.dot(v_blk[...], jnp.exp2(scores.pop(n_val)).astype(BF16),
                             preferred_element_type=F32)
                acc = oj if n_val % per_unit == 0 else acc + oj
                if n_val % per_unit == per_unit - 1:
                    emit(i, normalised(acc), 0, tq, stage, o_ref)

    def scores(i, par, c0):
        q = q_block(i, c0, cols)
        kr = pl.multiple_of(k_row(i), BF16_ROWS)
        m = None
        for j, k_ref in enumerate(k_refs):
            s = lax.dot_general(k_ref[0, pl.ds(kr, d), :], q, _TN, preferred_element_type=F32)
            s_bufs[par][bounds[j]:bounds[j + 1], :] = s
            mj = jnp.max(s, axis=0, keepdims=True)
            m = mj if m is None else jnp.maximum(m, mj)
        m_bufs[par][...] = jnp.broadcast_to(m, m_bufs[par].shape)

    def values(i, par, c0):
        m = m_bufs[par][0:1, :]
        vr = pl.multiple_of(v_row(i), BF16_ROWS)
        ox = None
        for j, v_ref in enumerate(v_refs):
            p = jnp.exp2(s_bufs[par][bounds[j]:bounds[j + 1], :] - m).astype(BF16)
            oj = jnp.dot(v_ref[0, pl.ds(vr, v_rows), :], p, preferred_element_type=F32)
            ox = oj if ox is None else ox + oj
        emit(i, normalised(ox), c0, cols, stage, o_ref)

    @pl.when(safe_ref[0] != 0)
    def _():
        if bounds[-1] <= FEW_KEYS:
            shift_free_units(0, N_UNITS, FEW_KEYS_LOOKAHEAD)
        else:
            def step(t, carry):
                shift_free_units(t * UNITS_PER_STEP, UNITS_PER_STEP, LOOKAHEAD)
                return carry

            lax.fori_loop(0, N_UNITS // UNITS_PER_STEP, step, 0)

    @pl.when(safe_ref[0] == 0)
    def _():
        for c0 in range(0, tq, cols):
            scores(0, 0, c0)

            def step(t, carry):
                @pl.when(t % 2 == 0)
                def _():
                    scores(t + 1, 1, c0)
                    values(t, 0, c0)

                @pl.when(t % 2 == 1)
                def _():
                    scores(t + 1, 0, c0)
                    values(t, 1, c0)
                return carry

            lax.fori_loop(0, N_UNITS - 1, step, 0)
            values(N_UNITS - 1, (N_UNITS - 1) % 2, c0)

    finish(stage, extra, o_ref)


def _attention(cfg, safe, batch, q_t, k_ts, v_ts, extras, extra_specs, tq, name):
    def per_batch(a):
        return a.shape[2] * a.shape[0] // batch

    def spec(a, width):
        blocks = per_batch(a) // width
        if a.shape[0] == batch:
            return pl.BlockSpec((1, a.shape[1], width), lambda i, t, safe_ref: (i, 0, t % blocks))
        return pl.BlockSpec((1, a.shape[1], width), lambda i, t, safe_ref: (0, 0, i * blocks + t % blocks))

    stage_shape = cfg[6](tq)
    cols = min(tq, EXACT_PATH_COLS)
    nk = sum(per_batch(k) for k in k_ts)
    out_shape = jax.ShapeDtypeStruct((q_t.shape[0], D_MODEL, q_t.shape[2]), BF16)
    return pl.pallas_call(
        functools.partial(_attention_kernel, cfg, len(k_ts)),
        grid_spec=pltpu.PrefetchScalarGridSpec(
            num_scalar_prefetch=1,
            grid=(batch, per_batch(q_t) // tq),
            in_specs=([spec(q_t, tq)] + [spec(a, per_batch(a)) for a in k_ts]
                      + [spec(a, per_batch(a)) for a in v_ts] + extra_specs),
            out_specs=spec(out_shape, tq),
            scratch_shapes=[pltpu.VMEM((nk, cols), F32), pltpu.VMEM((nk, cols), F32),
                            pltpu.VMEM((8, cols), F32), pltpu.VMEM((8, cols), F32),
                            pltpu.VMEM(stage_shape, F32)]),
        out_shape=out_shape,
        compiler_params=_params(2),
        name=name,
    )(safe, q_t, *k_ts, *v_ts, *extras)


def _even_attention_cfg(lam_init):
    def k_row(i):
        return jnp.where(i < 8, i * HEAD_DIM, 512 + ((i - 8) // GQA_GROUP) * HEAD_DIM)

    def v_row(i):
        return jnp.where(i < 8, i // 2, DIFF_HEADS + (i - 8) // GQA_GROUP) * EVEN_V_BLOCK

    def emit(i, o, c0, width, stage, o_ref):
        stage[i, :, c0:c0 + width] = o

    def finish(stage, extra, o_ref):
        dl_ref, sub_ref = extra
        dl = dl_ref[...]
        lam = (jnp.exp(jnp.sum(dl[0:1] * dl[1:2], axis=1, keepdims=True))
               - jnp.exp(jnp.sum(dl[2:3] * dl[3:4], axis=1, keepdims=True)) + lam_init)
        for h in range(DIFF_HEADS):
            o = stage[2 * h] - lam * stage[2 * h + 1]
            o = o * _rms_rows(o, DIFF_V_DIM) * sub_ref[...]
            o_ref[0, h * DIFF_V_DIM:(h + 1) * DIFF_V_DIM, :] = o.astype(BF16)
        for g in range(GQA_Q_HEADS):
            r = 512 + g * HEAD_DIM
            o_ref[0, r:r + HEAD_DIM, :] = stage[8 + g, 0:HEAD_DIM, :].astype(BF16)

    return (HEAD_DIM, EVEN_V_BLOCK, DIFF_V_DIM, lambda i: i * HEAD_DIM, k_row, v_row,
            lambda tq: (N_UNITS, DIFF_V_DIM, tq), emit, finish)


def _odd_attention_cfg():
    def emit(i, o, c0, width, stage, o_ref):
        o_ref[0, pl.ds(pl.multiple_of(i * MLA_V, BF16_ROWS), MLA_V), c0:c0 + width] = o.astype(BF16)

    def finish(stage, extra, o_ref):
        pass

    return (MLA_QK_DIM, ODD_V_BLOCK, MLA_V, lambda i: i * MLA_QK_DIM, lambda i: i * MLA_QK_DIM,
            lambda i: i * ODD_V_BLOCK, lambda tq: (8, 128), emit, finish)


def _out_mlp_kernel(x_ref, o_ref, mod_ref, g_ref, wo_ref, w1_ref, w2_ref, out_ref):
    y = lax.dot_general(o_ref[0], wo_ref[...], _TN, preferred_element_type=F32)
    x1 = x_ref[0] + mod_ref[2:3, :] * y
    h = _rms_modulate(x1, g_ref[...], mod_ref[3:4, :], mod_ref[4:5, :]).astype(BF16)
    acc = jnp.zeros(x1.shape, F32)
    for c in range(D_FF // FF_CHUNK):
        u = jnp.dot(h, w1_ref[:, c * FF_CHUNK:(c + 1) * FF_CHUNK], preferred_element_type=F32)
        a = jnp.square(jnp.maximum(u, 0.0)).astype(BF16)
        acc = acc + jnp.dot(a, w2_ref[c * FF_CHUNK:(c + 1) * FF_CHUNK, :], preferred_element_type=F32)
    out_ref[0] = x1 + mod_ref[5:6, :] * acc


def _out_mlp(x, o_t, mod, layer, ctx_row, g, wo, w1, w2):
    b, n, _ = x.shape
    tm = min(TOKEN_TILE, n)
    return pl.pallas_call(
        _out_mlp_kernel,
        grid=(b, n // tm),
        in_specs=[
            pl.BlockSpec((1, tm, D_MODEL), lambda i, t: (i, t, 0)),
            pl.BlockSpec((1, D_MODEL, tm), lambda i, t: (i, 0, t)),
            _mod_spec(layer, ctx_row),
            _const_spec((1, D_MODEL)),
            _const_spec((D_MODEL, D_MODEL)),
            _const_spec((D_MODEL, D_FF)),
            _const_spec((D_FF, D_MODEL)),
        ],
        out_specs=pl.BlockSpec((1, tm, D_MODEL), lambda i, t: (i, t, 0)),
        out_shape=jax.ShapeDtypeStruct((b, n, D_MODEL), F32),
        compiler_params=_params(2),
        name="outproj_mlp",
    )(x, o_t, mod, g, wo, w1, w2)


def _rope_tables(n_lat, rot_dim):
    rows = n_lat // GRID_W
    n_freq = rot_dim // 4
    inv_freq = ROPE_THETA ** (-jnp.arange(n_freq, dtype=F32) / n_freq)
    row = jnp.repeat(jnp.arange(rows, dtype=F32), GRID_W)
    col = jnp.tile(jnp.arange(GRID_W, dtype=F32), rows)
    ang = jnp.concatenate([row[:, None] * inv_freq, col[:, None] * inv_freq], axis=-1)
    return jnp.cos(ang).T, jnp.sin(ang).T


def _identity_rope(n_tokens, rot_dim):
    return jnp.ones((rot_dim // 2, n_tokens), F32), jnp.zeros((rot_dim // 2, n_tokens), F32)


def _col(v):
    return v.astype(F32)[:, None]


def _max_sq(gain):
    return jnp.max(jnp.square(gain.astype(F32)))


def kernel(x, c, ctx, c_ctx, ada_w, ada_b, norm_mix, norm_mlp, mlp_w1, mlp_w2, ab_w_in, ab_w_out,
           diff_qk_norm, diff_lambda, diff_subln, gqa_qk_norm, mla_w_in, mla_q_norm, mla_w_q_up,
           mla_kv_norm, mla_w_kv_up, mla_qk_norm, mla_w_out):
    b, n_lat, d = x.shape
    n_ctx = ctx.shape[1]
    assert d == D_MODEL and b < MOD_ROWS
    assert n_lat % QUERY_TILE == 0 and n_lat % TOKEN_TILE == 0
    n_fold = b * n_ctx
    assert n_fold % TOKEN_TILE == 0 or n_fold <= TOKEN_TILE

    mod = _modulation(c, c_ctx, ada_w, ada_b)
    rope_lat = {HEAD_DIM: _rope_tables(n_lat, HEAD_DIM), MLA_ROPE: _rope_tables(n_lat, MLA_ROPE)}
    rope_ctx = {HEAD_DIM: _identity_rope(n_fold, HEAD_DIM), MLA_ROPE: _identity_rope(n_fold, MLA_ROPE)}
    streams = {"lat": (None, rope_lat), "ctx": (b, rope_ctx)}
    xs = {"lat": x, "ctx": ctx.reshape(1, n_fold, D_MODEL)}

    for layer in range(DEPTH):
        update_ctx = layer < DEPTH - 1
        i = layer // 2
        g_mix = norm_mix[layer][None, :]
        qkv = {}
        if layer % 2 == 0:
            lam_init = _lambda_init(layer)
            ones = jnp.ones((HEAD_DIM,), F32)
            q_scale = QK_SCALE_64 * LOG2E
            gains = _col(jnp.concatenate([
                jnp.tile(diff_qk_norm[i, 0] * q_scale, 8), jnp.tile(diff_qk_norm[i, 1], 8),
                jnp.tile(ones, 8),
                jnp.tile(gqa_qk_norm[i, 0] * q_scale, 8), jnp.tile(gqa_qk_norm[i, 1], 2),
                jnp.tile(ones, 2)]))
            bound2 = HEAD_DIM ** 2 * jnp.maximum(
                _max_sq(diff_qk_norm[i, 0] * q_scale) * _max_sq(diff_qk_norm[i, 1]),
                _max_sq(gqa_qk_norm[i, 0] * q_scale) * _max_sq(gqa_qk_norm[i, 1]))
            w_t = ab_w_in[i].T.astype(BF16)
            for name, (ctx_row, rope) in streams.items():
                qkv[name] = _inproj_even(xs[name], mod, layer, ctx_row, g_mix, w_t, gains, *rope[HEAD_DIM])
            cfg = _even_attention_cfg(lam_init)
            extras = [diff_lambda[i], _col(diff_subln[i] * (1.0 - lam_init))]
            extra_specs = [_const_spec((4, HEAD_DIM)), _const_spec((DIFF_V_DIM, 1))]
            w_out = ab_w_out[i]
        else:
            q_gain = mla_qk_norm[i, 0] * (MLA_SCALE * LOG2E)
            bound2 = MLA_QK_DIM ** 2 * _max_sq(q_gain) * _max_sq(mla_qk_norm[i, 1])
            weights = (mla_w_in[i].T.astype(BF16), _col(mla_q_norm[i]), mla_w_q_up[i].T.astype(BF16),
                       _col(mla_kv_norm[i]), mla_w_kv_up[i].T.astype(BF16),
                       _col(q_gain), _col(mla_qk_norm[i, 1]))
            for name, (ctx_row, rope) in streams.items():
                qkv[name] = _inproj_odd(xs[name], mod, layer, ctx_row, g_mix, *weights, *rope[MLA_ROPE])
            cfg = _odd_attention_cfg()
            extras, extra_specs = [], []
            w_out = mla_w_out[i]

        safe = (bound2 * NORM_MARGIN <= MAX_ABS_SCORE ** 2).astype(jnp.int32).reshape(1)
        o = {"lat": _attention(cfg, safe, b, qkv["lat"][0], [qkv["lat"][1], qkv["ctx"][1]],
                               [qkv["lat"][2], qkv["ctx"][2]], extras, extra_specs, QUERY_TILE,
                               "attention_lat")}
        if update_ctx:
            o["ctx"] = _attention(cfg, safe, b, qkv["ctx"][0], [qkv["ctx"][1]], [qkv["ctx"][2]], extras,
                                  extra_specs, n_ctx, "attention_ctx")
        wo, w1, w2 = w_out.astype(BF16), mlp_w1[layer].astype(BF16), mlp_w2[layer].astype(BF16)
        for name in o:
            xs[name] = _out_mlp(xs[name], o[name], mod, layer, streams[name][0],
                                norm_mlp[layer][None, :], wo, w1, w2)
    return xs["lat"]
```
